```python
import jax, jax.numpy as jnp
from jax import lax
import numpy as np

D_MODEL = 1024
BATCH = 8
SEQ = 2048
DEPTH = 4

HEAD_DIM = 64
A_HEADS = 8
A_WIDTH = A_HEADS * HEAD_DIM
B_WIDTH = 512
C_GROUPS = 4
C_WIDTH = 512
D_HEADS = 8
D_WIDTH = D_HEADS * HEAD_DIM
MOBA_BLOCK = 256
MOBA_TOPK = 3
MOBA_Q_CHUNK = 128
CONV_K = 31
SGU_CHUNK = 128
SB_Q_BLOCK = 128
D_FF = 2816
ROPE_THETA = 10000.0
RMS_EPS = 1e-6
LN_EPS = 1e-5
N_EVEN = (DEPTH + 1) // 2
N_ODD = DEPTH // 2
AB_IN = 3 * A_WIDTH + 2 * B_WIDTH
CD_IN = 2 * C_WIDTH + 3 * D_WIDTH

kernel_name = "hybrid_moba_conv_gmlp_stickbreak_trunk"


def rmsnorm(x, g):
    xf = x.astype(jnp.float32)
    y = xf * lax.rsqrt(jnp.mean(xf * xf, axis=-1, keepdims=True) + RMS_EPS)
    return (y * g.astype(jnp.float32)).astype(x.dtype)


def layernorm(x, g, b):
    xf = x.astype(jnp.float32)
    mu = jnp.mean(xf, axis=-1, keepdims=True)
    var = jnp.mean(jnp.square(xf - mu), axis=-1, keepdims=True)
    y = (xf - mu) * lax.rsqrt(var + LN_EPS)
    return (y * g.astype(jnp.float32) + b.astype(jnp.float32)).astype(x.dtype)


def swiglu(h, w_gate, w_up, w_down):
    return (jax.nn.silu(h @ w_gate) * (h @ w_up)) @ w_down


def rope_tables(seq):
    pos = jnp.arange(seq, dtype=jnp.float32)
    inv = ROPE_THETA ** (-jnp.arange(0, HEAD_DIM, 2, dtype=jnp.float32) / HEAD_DIM)
    ang = pos[:, None] * inv[None, :]
    return jnp.cos(ang), jnp.sin(ang)


def apply_rope(x, cos, sin):
    half = HEAD_DIM // 2
    xf = x.astype(jnp.float32)
    x1, x2 = xf[..., :half], xf[..., half:]
    c, s = cos[None, :, None, :], sin[None, :, None, :]
    return jnp.concatenate([x1 * c - x2 * s, x2 * c + x1 * s], axis=-1).astype(x.dtype)


def moba_attention(q, k, v):
    bsz, seq, nh, dh = q.shape
    n_blk = -(-seq // MOBA_BLOCK)
    pad = n_blk * MOBA_BLOCK - seq
    kp = jnp.pad(k, ((0, 0), (0, pad), (0, 0), (0, 0)))
    vp = jnp.pad(v, ((0, 0), (0, pad), (0, 0), (0, 0)))
    k_blocks = kp.reshape(bsz, n_blk, MOBA_BLOCK, nh, dh).transpose(0, 3, 1, 2, 4)
    v_blocks = vp.reshape(bsz, n_blk, MOBA_BLOCK, nh, dh).transpose(0, 3, 1, 2, 4)
    k_mean = jnp.mean(k_blocks.astype(jnp.float32), axis=3).astype(k.dtype)
    qh = q.transpose(0, 2, 1, 3)
    n_qc = seq // MOBA_Q_CHUNK
    top = min(MOBA_TOPK, n_blk)
    scale = HEAD_DIM ** -0.5
    blk_ids = jnp.arange(n_blk)
    gather_blocks = jax.vmap(lambda kb_h, idx_h: kb_h[idx_h])

    def one_chunk(bc):
        b, c = bc
        start = c * MOBA_Q_CHUNK
        q_c = lax.dynamic_slice_in_dim(qh[b], start, MOBA_Q_CHUNK, axis=1)
        kb, vb = k_blocks[b], v_blocks[b]
        own = start // MOBA_BLOCK
        q_pos = start + jnp.arange(MOBA_Q_CHUNK)
        gate = jnp.einsum('hqd,hnd->hqn', q_c, k_mean[b]).astype(jnp.float32)
        gate = jnp.where(blk_ids[None, None, :] < own, gate, -jnp.inf)
        _, idx = lax.top_k(gate, top)
        valid = idx < own
        k_sel = gather_blocks(kb, idx)
        v_sel = gather_blocks(vb, idx)
        s_sel = jnp.einsum('hqd,hqjpd->hqjp', q_c, k_sel).astype(jnp.float32) * scale
        s_sel = jnp.where(valid[..., None], s_sel, -jnp.inf).reshape(nh, MOBA_Q_CHUNK, top * MOBA_BLOCK)
        k_own = lax.dynamic_index_in_dim(kb, own, axis=1, keepdims=False)
        v_own = lax.dynamic_index_in_dim(vb, own, axis=1, keepdims=False)
        s_own = jnp.einsum('hqd,hpd->hqp', q_c, k_own).astype(jnp.float32) * scale
        key_pos = own * MOBA_BLOCK + jnp.arange(MOBA_BLOCK)
        s_own = jnp.where(key_pos[None, None, :] <= q_pos[None, :, None], s_own, -jnp.inf)
        p = jax.nn.softmax(jnp.concatenate([s_sel, s_own], axis=-1), axis=-1).astype(v.dtype)
        p_sel = p[..., :top * MOBA_BLOCK].reshape(nh, MOBA_Q_CHUNK, top, MOBA_BLOCK)
        p_own = p[..., top * MOBA_BLOCK:]
        return (jnp.einsum('hqjp,hqjpd->hqd', p_sel, v_sel)
                + jnp.einsum('hqp,hpd->hqd', p_own, v_own))

    b_ids = jnp.repeat(jnp.arange(bsz), n_qc)
    c_ids = jnp.tile(jnp.arange(n_qc), bsz)
    o = lax.map(one_chunk, (b_ids, c_ids))
    o = o.reshape(bsz, n_qc, nh, MOBA_Q_CHUNK, dh).transpose(0, 1, 3, 2, 4)
    return o.reshape(bsz, seq, nh * dh)


def conformer_conv(a, g, conv_w, conv_b, ln_g, ln_b):
    h = a * jax.nn.sigmoid(g)
    h = lax.conv_general_dilated(
        h, conv_w[:, None, :].astype(h.dtype), window_strides=(1,),
        padding=[(CONV_K - 1, 0)], dimension_numbers=('NWC', 'WIO', 'NWC'),
        feature_group_count=B_WIDTH) + conv_b
    return jax.nn.silu(layernorm(h, ln_g, ln_b))


def chunked_sgu(u, v, ln_g, ln_b, w_s, b_s):
    u = jax.nn.gelu(u)
    v = layernorm(jax.nn.gelu(v), ln_g, ln_b)
    bsz, seq, ch = v.shape
    vg = v.reshape(bsz, seq // SGU_CHUNK, SGU_CHUNK, C_GROUPS, ch // C_GROUPS)
    causal = jnp.tril(jnp.ones((SGU_CHUNK, SGU_CHUNK), dtype=w_s.dtype))
    mixed = jnp.einsum('gts,bnsgc->bntgc', w_s * causal, vg) + b_s.T[None, None, :, :, None]
    return u * mixed.reshape(bsz, seq, ch)


def stick_breaking_attention(q, k, v):
    bsz, seq, nh, dh = q.shape
    qh, kh, vh = (t.transpose(0, 2, 1, 3) for t in (q, k, v))
    scale = HEAD_DIM ** -0.5
    s_pos = jnp.arange(seq)
    n_qb = seq // SB_Q_BLOCK

    def one_block(i):
        start = i * SB_Q_BLOCK
        qb = lax.dynamic_slice_in_dim(qh, start, SB_Q_BLOCK, axis=2)
        z = jnp.einsum('bhqd,bhkd->bhqk', qb, kh).astype(jnp.float32) * scale
        t_pos = start + jnp.arange(SB_Q_BLOCK)
        strict = s_pos[None, :] < t_pos[:, None]
        log_1m = jnp.where(strict, jax.nn.log_sigmoid(-z), 0.0)
        after = lax.cumsum(log_1m, axis=3, reverse=True) - log_1m
        log_a = jnp.where(strict, jax.nn.log_sigmoid(z) + after, -jnp.inf)
        att = jnp.exp(log_a).astype(v.dtype)
        return jnp.einsum('bhqk,bhkd->bhqd', att, vh)

    o = lax.map(one_block, jnp.arange(n_qb))
    return o.transpose(1, 0, 3, 2, 4).reshape(bsz, seq, nh * dh)


def even_mixer(h, w_in, w_out, conv_w, conv_b, ln_g, ln_b, cos, sin):
    bsz, seq, _ = h.shape
    proj = h @ w_in
    q, k, v, ga, gb = jnp.split(proj, [A_WIDTH, 2 * A_WIDTH, 3 * A_WIDTH, 3 * A_WIDTH + B_WIDTH], axis=-1)
    q = apply_rope(q.reshape(bsz, seq, A_HEADS, HEAD_DIM), cos, sin)
    k = apply_rope(k.reshape(bsz, seq, A_HEADS, HEAD_DIM), cos, sin)
    v = v.reshape(bsz, seq, A_HEADS, HEAD_DIM)
    o_a = moba_attention(q, k, v)
    o_b = conformer_conv(ga, gb, conv_w, conv_b, ln_g, ln_b)
    return jnp.concatenate([o_a, o_b], axis=-1) @ w_out


def odd_mixer(h, w_in, w_out, ln_g, ln_b, w_s, b_s):
    bsz, seq, _ = h.shape
    proj = h @ w_in
    u, vc, q, k, v = jnp.split(proj, [C_WIDTH, 2 * C_WIDTH, 2 * C_WIDTH + D_WIDTH, 2 * C_WIDTH + 2 * D_WIDTH], axis=-1)
    o_c = chunked_sgu(u, vc, ln_g, ln_b, w_s, b_s)
    o_d = stick_breaking_attention(q.reshape(bsz, seq, D_HEADS, HEAD_DIM),
                                   k.reshape(bsz, seq, D_HEADS, HEAD_DIM),
                                   v.reshape(bsz, seq, D_HEADS, HEAD_DIM))
    return jnp.concatenate([o_c, o_d], axis=-1) @ w_out


def setup_inputs(seed: int = 0) -> dict:
    key = jax.random.key(seed)
    ks = jax.random.split(key, 17)
    f32 = jnp.float32
    nrm = lambda k, shape, s: jax.random.normal(k, shape, f32) * s
    return {
        'x': jax.random.normal(ks[0], (BATCH, SEQ, D_MODEL), f32),
        'norm_g': 1.0 + nrm(ks[1], (DEPTH, 6, D_MODEL), 0.02),
        'ffn_w_gate': nrm(ks[2], (DEPTH, 2, D_MODEL, D_FF), D_MODEL ** -0.5),
        'ffn_w_up': nrm(ks[3], (DEPTH, 2, D_MODEL, D_FF), D_MODEL ** -0.5),
        'ffn_w_down': nrm(ks[4], (DEPTH, 2, D_FF, D_MODEL), D_FF ** -0.5),
        'ab_w_in': nrm(ks[5], (N_EVEN, D_MODEL, AB_IN), D_MODEL ** -0.5),
        'ab_w_out': nrm(ks[6], (N_EVEN, A_WIDTH + B_WIDTH, D_MODEL), (A_WIDTH + B_WIDTH) ** -0.5),
        'conv_w': nrm(ks[7], (N_EVEN, CONV_K, B_WIDTH), CONV_K ** -0.5),
        'conv_b': nrm(ks[8], (N_EVEN, B_WIDTH), 0.02),
        'conv_ln_g': 1.0 + nrm(ks[9], (N_EVEN, B_WIDTH), 0.02),
        'conv_ln_b': nrm(ks[10], (N_EVEN, B_WIDTH), 0.02),
        'cd_w_in': nrm(ks[11], (N_ODD, D_MODEL, CD_IN), D_MODEL ** -0.5),
        'cd_w_out': nrm(ks[12], (N_ODD, C_WIDTH + D_WIDTH, D_MODEL), (C_WIDTH + D_WIDTH) ** -0.5),
        'sgu_ln_g': 1.0 + nrm(ks[13], (N_ODD, C_WIDTH), 0.02),
        'sgu_ln_b': nrm(ks[14], (N_ODD, C_WIDTH), 0.02),
        'sgu_w': nrm(ks[15], (N_ODD, C_GROUPS, SGU_CHUNK, SGU_CHUNK), SGU_CHUNK ** -0.5),
        'sgu_b': 1.0 + nrm(ks[16], (N_ODD, C_GROUPS, SGU_CHUNK), 0.02),
    }


def reference(x, norm_g, ffn_w_gate, ffn_w_up, ffn_w_down, ab_w_in, ab_w_out,
              conv_w, conv_b, conv_ln_g, conv_ln_b, cd_w_in, cd_w_out,
              sgu_ln_g, sgu_ln_b, sgu_w, sgu_b):
    seq = x.shape[1]
    cos, sin = rope_tables(seq)
    for layer in range(DEPTH):
        g = norm_g[layer]
        f = swiglu(rmsnorm(x, g[0]), ffn_w_gate[layer, 0], ffn_w_up[layer, 0], ffn_w_down[layer, 0])
        x = x + 0.5 * rmsnorm(f, g[1])
        h = rmsnorm(x, g[2])
        i = layer // 2
        if layer % 2 == 0:
            m = even_mixer(h, ab_w_in[i], ab_w_out[i], conv_w[i], conv_b[i],
                           conv_ln_g[i], conv_ln_b[i], cos, sin)
        else:
            m = odd_mixer(h, cd_w_in[i], cd_w_out[i], sgu_ln_g[i], sgu_ln_b[i], sgu_w[i], sgu_b[i])
        x = x + rmsnorm(m, g[3])
        f = swiglu(rmsnorm(x, g[4]), ffn_w_gate[layer, 1], ffn_w_up[layer, 1], ffn_w_down[layer, 1])
        x = x + 0.5 * rmsnorm(f, g[5])
    return x
```

```python
import functools

import jax
import jax.numpy as jnp
from jax import lax
from jax.experimental import pallas as pl
from jax.experimental.pallas import tpu as pltpu

HEAD_DIM = 64
MOBA_BLOCK = 256
MOBA_TOPK = 3
CONV_K = 31
SGU_CHUNK = 128
C_GROUPS = 4
ROPE_THETA = 10000.0
RMS_EPS = 1e-6
LN_EPS = 1e-5

LANES = 128
CONV_HALO = 32
VMEM_LIMIT = 56 * 1024 * 1024

F32 = jnp.float32
BF16 = jnp.bfloat16


def _rms(x, g):
    return x * lax.rsqrt(jnp.mean(x * x, axis=-1, keepdims=True) + RMS_EPS) * g


def _layernorm(x, g, b):
    mu = jnp.mean(x, axis=-1, keepdims=True)
    xc = x - mu
    var = jnp.mean(xc * xc, axis=-1, keepdims=True)
    return xc * lax.rsqrt(var + LN_EPS) * g + b


def _sigmoid(x):
    return 1.0 / (1.0 + jnp.exp(-x))


def _gelu_tanh(x):
    c = 0.7978845608028654
    return 0.5 * x * (1.0 + jnp.tanh(c * (x + 0.044715 * (x * x * x))))


def _resident(shape):
    nd = len(shape)
    return pl.BlockSpec(shape, lambda *_: (0,) * nd, pipeline_mode=pl.Buffered(1))


def _params(n_grid):
    return pltpu.CompilerParams(
        dimension_semantics=("arbitrary",) * n_grid, vmem_limit_bytes=VMEM_LIMIT)


def _ffn_kernel(x_ref, g_ref, wg_ref, wu_ref, wd_ref, o_ref, act_ref, *, g_in, g_out, tf):
    x = x_ref[...]
    h = _rms(x, g_ref[g_in:g_in + 1, :]).astype(BF16)
    d_ff = wg_ref.shape[1]
    for c in range(d_ff // tf):
        sl = slice(c * tf, (c + 1) * tf)
        gate = jnp.dot(h, wg_ref[:, sl], preferred_element_type=F32)
        up = jnp.dot(h, wu_ref[:, sl], preferred_element_type=F32)
        act_ref[:, sl] = (gate * _sigmoid(gate) * up).astype(BF16)
    f = jnp.dot(act_ref[...], wd_ref[...], preferred_element_type=F32)
    o_ref[...] = x + 0.5 * _rms(f, g_ref[g_out:g_out + 1, :])


def _ffn(x2, g, wg, wu, wd, g_in, g_out, tm=512, tf=256):
    t, d = x2.shape
    d_ff = wg.shape[1]
    return pl.pallas_call(
        functools.partial(_ffn_kernel, g_in=g_in, g_out=g_out, tf=tf),
        grid=(t // tm,),
        in_specs=[
            pl.BlockSpec((tm, d), lambda i: (i, 0)),
            _resident(g.shape),
            _resident(wg.shape),
            _resident(wu.shape),
            _resident(wd.shape),
        ],
        out_specs=pl.BlockSpec((tm, d), lambda i: (i, 0)),
        out_shape=jax.ShapeDtypeStruct((t, d), F32),
        scratch_shapes=[pltpu.VMEM((tm, d_ff), BF16)],
        compiler_params=_params(1),
        name="ffn",
    )(x2, g, wg, wu, wd)


def _rope(x, cos, sin_signed):
    half = HEAD_DIM // 2
    lane = lax.broadcasted_iota(jnp.int32, x.shape, 1)
    first_half = (lane % HEAD_DIM) < half
    partner = jnp.where(first_half, pltpu.roll(x, LANES - half, 1), pltpu.roll(x, half, 1))
    return x * cos + partner * sin_signed


def _inproj_kernel(x_ref, g_ref, w_ref, *rest, rope, width):
    if rope:
        cos_ref, sin_ref = rest[:2]
        outs = rest[2:]
    else:
        outs = rest
    h = _rms(x_ref[...], g_ref[2:3, :]).astype(BF16)
    for n, o_ref in enumerate(outs):
        y = jnp.dot(h, w_ref[:, n * width:(n + 1) * width], preferred_element_type=F32)
        if rope and n < 2:
            for t in range(width // LANES):
                sl = slice(t * LANES, (t + 1) * LANES)
                o_ref[:, sl] = _rope(y[:, sl], cos_ref[...], sin_ref[...]).astype(o_ref.dtype)
        else:
            o_ref[...] = y.astype(o_ref.dtype)


def _inproj(x2, g, w, seq, rope_tabs, out_dtypes, tm=512):
    t, d = x2.shape
    width = w.shape[1] // len(out_dtypes)
    n_seq = seq // tm
    in_specs = [
        pl.BlockSpec((tm, d), lambda i: (i, 0)),
        _resident(g.shape),
        _resident(w.shape),
    ]
    args = [x2, g, w]
    if rope_tabs is not None:
        in_specs += [pl.BlockSpec((tm, LANES), lambda i: (i % n_seq, 0))] * 2
        args += list(rope_tabs)
    return pl.pallas_call(
        functools.partial(_inproj_kernel, rope=rope_tabs is not None, width=width),
        grid=(t // tm,),
        in_specs=in_specs,
        out_specs=[pl.BlockSpec((tm, width), lambda i: (i, 0))] * len(out_dtypes),
        out_shape=[jax.ShapeDtypeStruct((t, width), dt) for dt in out_dtypes],
        compiler_params=_params(1),
        name="inproj_rope" if rope_tabs is not None else "inproj",
    )(*args)


def _moba_select_bias(gate, own):
    lane = lax.broadcasted_iota(jnp.int32, gate.shape, 1)
    past = lane < own
    gm = jnp.where(past, gate, -jnp.inf)
    rank = jnp.zeros(gate.shape, jnp.int32)
    for i in range(own):
        gi = gm[:, i:i + 1]
        beats = (gi > gm) | ((gi == gm) & (lane > i))
        rank = rank + jnp.where(beats, 1, 0)
    return jnp.where(past & (rank < MOBA_TOPK), 0.0, -jnp.inf)


def _moba_kernel(q_ref, k_ref, v_ref, o_ref, qm_ref, kb_ref, vb_ref, km_ref):
    seq = q_ref.shape[1]
    blk = MOBA_BLOCK
    n_blk = seq // blk
    scale = HEAD_DIM ** -0.5
    kb_ref[...] = k_ref[0].astype(BF16)
    vb_ref[...] = v_ref[0].astype(BF16)
    km_ref[...] = jnp.zeros(km_ref.shape, F32)
    for j in range(n_blk):
        km_ref[j:j + 1, :] = jnp.mean(k_ref[0, j * blk:(j + 1) * blk, :], axis=0, keepdims=True)
    lane = lax.broadcasted_iota(jnp.int32, (1, LANES), 1)
    for hh in range(2):
        qm_ref[hh] = jnp.where(lane // HEAD_DIM == hh, q_ref[0], 0.0)
    row = lax.broadcasted_iota(jnp.int32, (blk, blk), 0)
    col = lax.broadcasted_iota(jnp.int32, (blk, blk), 1)
    causal_bias = jnp.where(col <= row, 0.0, -jnp.inf)
    for i in range(n_blk):
        n_keys = (i + 1) * blk
        pv = []
        for hh in range(2):
            qm = qm_ref[hh, i * blk:(i + 1) * blk, :]
            s = lax.dot_general((qm * scale).astype(BF16), kb_ref[0:n_keys, :],
                                (((1,), (1,)), ((), ())), preferred_element_type=F32)
            pieces = [s[:, j * blk:(j + 1) * blk] for j in range(i + 1)]
            if i > MOBA_TOPK:
                gate = lax.dot_general(qm, km_ref[...], (((1,), (1,)), ((), ())),
                                       precision=lax.Precision.HIGHEST,
                                       preferred_element_type=F32)
                bias = _moba_select_bias(gate, i)
                pieces = [pieces[j] + bias[:, j:j + 1] for j in range(i)] + pieces[i:]
            pieces[i] = pieces[i] + causal_bias
            m = pieces[0]
            for piece in pieces[1:]:
                m = jnp.maximum(m, piece)
            m = jnp.max(m, axis=1, keepdims=True)
            p = jnp.concatenate([jnp.exp(piece - m) for piece in pieces], axis=1)
            denom = jnp.sum(p, axis=1, keepdims=True)
            o = jnp.dot(p.astype(BF16), vb_ref[0:n_keys, :], preferred_element_type=F32)
            pv.append(o * (1.0 / denom))
        o_ref[0, i * blk:(i + 1) * blk, :] = jnp.where(lane < HEAD_DIM, pv[0], pv[1])


def _neg_softplus(z):
    return -(jnp.maximum(z, 0.0) + jnp.log(1.0 + jnp.exp(-jnp.abs(z))))


def _suffix_sums(x, tri):
    hi = x.astype(BF16)
    lo = (x - hi.astype(F32)).astype(BF16)
    return (jnp.dot(hi, tri, preferred_element_type=F32)
            + jnp.dot(lo, tri, preferred_element_type=F32))


def _stick_kernel(q_ref, k_ref, v_ref, o_ref, qm_ref, kb_ref, vb_ref, *, tq):
    seq = q_ref.shape[1]
    n_q = seq // tq
    scale = HEAD_DIM ** -0.5
    kb_ref[...] = k_ref[0].astype(BF16)
    vb_ref[...] = v_ref[0].astype(BF16)
    lane = lax.broadcasted_iota(jnp.int32, (1, LANES), 1)
    q = q_ref[0] * scale
    for hh in range(2):
        qm_ref[hh] = jnp.where(lane // HEAD_DIM == hh, q, 0.0).astype(BF16)
    row = lax.broadcasted_iota(jnp.int32, (tq, tq), 0)
    col = lax.broadcasted_iota(jnp.int32, (tq, tq), 1)
    strict = col < row
    tri = (row > col).astype(BF16)

    def tile(qm, c, carry, diag):
        start = pl.multiple_of(c * tq, tq)
        kc = kb_ref[pl.ds(start, tq), :]
        vc = vb_ref[pl.ds(start, tq), :]
        z = lax.dot_general(qm, kc, (((1,), (1,)), ((), ())), preferred_element_type=F32)
        log_1m = _neg_softplus(z)
        if diag:
            log_1m = jnp.where(strict, log_1m, 0.0)
        after = _suffix_sums(log_1m, tri) + carry
        log_a = z + log_1m + after
        att = jnp.exp(log_a)
        if diag:
            att = jnp.where(strict, att, 0.0)
        out = jnp.dot(att.astype(BF16), vc, preferred_element_type=F32)
        return out, carry + jnp.sum(log_1m, axis=1, keepdims=True)

    def q_block(i, _):
        q_start = pl.multiple_of(i * tq, tq)
        outs = []
        for hh in range(2):
            qm = qm_ref[hh, pl.ds(q_start, tq), :]
            acc, carry = tile(qm, i, jnp.zeros((tq, 1), F32), True)

            def k_block(step, state, qm=qm):
                acc, carry = state
                out, carry = tile(qm, i - 1 - step, carry, False)
                return acc + out, carry

            acc, _ = lax.fori_loop(0, i, k_block, (acc, carry))
            outs.append(acc)
        o_ref[0, pl.ds(q_start, tq), :] = jnp.where(lane < HEAD_DIM, outs[0], outs[1])
        return 0

    lax.fori_loop(0, n_q, q_block, 0)


def _attention(kind, q, k, v, bsz, seq):
    width = q.shape[1]
    q3, k3, v3 = (a.reshape(bsz, seq, width) for a in (q, k, v))
    spec = pl.BlockSpec((1, seq, LANES), lambda b, p: (b, 0, p))
    if kind == "moba":
        body = _moba_kernel
        scratch = [pltpu.VMEM((2, seq, LANES), F32),
                   pltpu.VMEM((seq, LANES), BF16),
                   pltpu.VMEM((seq, LANES), BF16),
                   pltpu.VMEM((LANES, LANES), F32)]
    else:
        body = functools.partial(_stick_kernel, tq=256)
        scratch = [pltpu.VMEM((2, seq, LANES), BF16),
                   pltpu.VMEM((seq, LANES), BF16),
                   pltpu.VMEM((seq, LANES), BF16)]
    out = pl.pallas_call(
        body,
        grid=(bsz, width // LANES),
        in_specs=[spec, spec, spec],
        out_specs=spec,
        out_shape=jax.ShapeDtypeStruct((bsz, seq, width), F32),
        scratch_shapes=scratch,
        compiler_params=_params(2),
        name=kind,
    )(q3, k3, v3)
    return out.reshape(bsz * seq, width)


def _conv_out_kernel(x_ref, oa_ref, ga_ref, gb_ref, hga_ref, hgb_ref, g_ref, cw_ref, cvec_ref,
                     w_ref, o_ref, ext_ref):
    ts = x_ref.shape[0]
    first = pl.program_id(1) == 0
    halo = hga_ref[...] * _sigmoid(hgb_ref[...])
    ext_ref[0:CONV_HALO, :] = jnp.where(first, 0.0, halo)
    ext_ref[CONV_HALO:CONV_HALO + ts, :] = ga_ref[...] * _sigmoid(gb_ref[...])
    off = CONV_HALO - (CONV_K - 1)
    acc = jnp.zeros((ts, ga_ref.shape[1]), F32) + cvec_ref[0:1, :]
    for tap in range(CONV_K):
        acc = acc + ext_ref[off + tap:off + tap + ts, :] * cw_ref[tap:tap + 1, :]
    y = _layernorm(acc, cvec_ref[1:2, :], cvec_ref[2:3, :])
    ob = y * _sigmoid(y)
    wa = oa_ref.shape[1]
    m = (jnp.dot(oa_ref[...].astype(BF16), w_ref[0:wa, :], preferred_element_type=F32)
         + jnp.dot(ob.astype(BF16), w_ref[wa:, :], preferred_element_type=F32))
    o_ref[...] = x_ref[...] + _rms(m, g_ref[3:4, :])


def _conv_out(x2, oa, ga, gb, g, conv_w, conv_vecs, w_out, bsz, seq, ts=512):
    t, d = x2.shape
    wb = ga.shape[1]
    n_seq = seq // ts
    halo_per_tile = ts // CONV_HALO
    tile = lambda width: pl.BlockSpec((ts, width), lambda b, i: (b * n_seq + i, 0))
    halo = pl.BlockSpec(
        (CONV_HALO, wb),
        lambda b, i: (jnp.maximum((b * n_seq + i) * halo_per_tile - 1, 0), 0))
    return pl.pallas_call(
        _conv_out_kernel,
        grid=(bsz, n_seq),
        in_specs=[tile(d), tile(oa.shape[1]), tile(wb), tile(wb), halo, halo,
                  _resident(g.shape), _resident(conv_w.shape), _resident(conv_vecs.shape),
                  _resident(w_out.shape)],
        out_specs=tile(d),
        out_shape=jax.ShapeDtypeStruct((t, d), F32),
        scratch_shapes=[pltpu.VMEM((CONV_HALO + ts, wb), F32)],
        compiler_params=_params(2),
        name="conv_out",
    )(x2, oa, ga, gb, ga, gb, g, conv_w, conv_vecs, w_out)


def _sgu_out_kernel(x_ref, u_ref, vc_ref, od_ref, g_ref, lnv_ref, ws_ref, bs_ref, w_ref, o_ref,
                    oc_ref):
    ts = x_ref.shape[0]
    n_chunk = ts // SGU_CHUNK
    gw = vc_ref.shape[1] // C_GROUPS
    v = _layernorm(_gelu_tanh(vc_ref[...]), lnv_ref[0:1, :], lnv_ref[1:2, :]).astype(BF16)
    row = lax.broadcasted_iota(jnp.int32, (SGU_CHUNK, SGU_CHUNK), 0)
    col = lax.broadcasted_iota(jnp.int32, (SGU_CHUNK, SGU_CHUNK), 1)
    for grp in range(C_GROUPS):
        w_s = jnp.where(col <= row, ws_ref[grp], 0.0).astype(BF16)
        lanes = slice(grp * gw, (grp + 1) * gw)
        vg = jnp.concatenate(
            [v[n * SGU_CHUNK:(n + 1) * SGU_CHUNK, lanes] for n in range(n_chunk)], axis=1)
        mixed = jnp.dot(w_s, vg, preferred_element_type=F32) + bs_ref[:, grp:grp + 1]
        for n in range(n_chunk):
            rows = slice(n * SGU_CHUNK, (n + 1) * SGU_CHUNK)
            oc_ref[rows, lanes] = (_gelu_tanh(u_ref[rows, lanes])
                                   * mixed[:, n * gw:(n + 1) * gw]).astype(BF16)
    wc = u_ref.shape[1]
    m = (jnp.dot(oc_ref[...], w_ref[0:wc, :], preferred_element_type=F32)
         + jnp.dot(od_ref[...].astype(BF16), w_ref[wc:, :], preferred_element_type=F32))
    o_ref[...] = x_ref[...] + _rms(m, g_ref[3:4, :])


def _sgu_out(x2, u, vc, od, g, ln_vecs, w_s, b_s_t, w_out, ts=512):
    t, d = x2.shape
    wc = u.shape[1]
    tile = lambda width: pl.BlockSpec((ts, width), lambda i: (i, 0))
    return pl.pallas_call(
        _sgu_out_kernel,
        grid=(t // ts,),
        in_specs=[tile(d), tile(wc), tile(wc), tile(od.shape[1]),
                  _resident(g.shape), _resident(ln_vecs.shape), _resident(w_s.shape),
                  _resident(b_s_t.shape), _resident(w_out.shape)],
        out_specs=tile(d),
        out_shape=jax.ShapeDtypeStruct((t, d), F32),
        scratch_shapes=[pltpu.VMEM((ts, wc), BF16)],
        compiler_params=_params(1),
        name="sgu_out",
    )(x2, u, vc, od, g, ln_vecs, w_s, b_s_t, w_out)


def _rope_tables(seq):
    half = HEAD_DIM // 2
    pos = jnp.arange(seq, dtype=F32)
    inv = ROPE_THETA ** (-jnp.arange(0, HEAD_DIM, 2, dtype=F32) / HEAD_DIM)
    ang = pos[:, None] * inv[None, :]
    cos, sin = jnp.cos(ang), jnp.sin(ang)
    reps = LANES // HEAD_DIM
    cos_t = jnp.tile(jnp.concatenate([cos, cos], axis=1), (1, reps))
    sin_t = jnp.tile(jnp.concatenate([-sin, sin], axis=1), (1, reps))
    return cos_t, sin_t


def kernel(x, norm_g, ffn_w_gate, ffn_w_up, ffn_w_down, ab_w_in, ab_w_out, conv_w, conv_b,
           conv_ln_g, conv_ln_b, cd_w_in, cd_w_out, sgu_ln_g, sgu_ln_b, sgu_w, sgu_b):
    bsz, seq, d = x.shape
    depth = norm_g.shape[0]
    rope_tabs = _rope_tables(seq)
    x2 = x.reshape(bsz * seq, d)
    for layer in range(depth):
        g = norm_g[layer]
        i = layer // 2
        wg, wu, wd = (w[layer].astype(BF16) for w in (ffn_w_gate, ffn_w_up, ffn_w_down))
        x2 = _ffn(x2, g, wg[0], wu[0], wd[0], 0, 1)
        if layer % 2 == 0:
            q, k, v, ga, gb = _inproj(x2, g, ab_w_in[i].astype(BF16), seq, rope_tabs, [F32] * 5)
            oa = _attention("moba", q, k, v, bsz, seq)
            conv_vecs = jnp.stack([conv_b[i], conv_ln_g[i], conv_ln_b[i]])
            x2 = _conv_out(x2, oa, ga, gb, g, conv_w[i], conv_vecs, ab_w_out[i].astype(BF16),
                           bsz, seq)
        else:
            u, vc, q, k, v = _inproj(x2, g, cd_w_in[i].astype(BF16), seq, None, [F32] * 5)
            od = _attention("stick", q, k, v, bsz, seq)
            ln_vecs = jnp.stack([sgu_ln_g[i], sgu_ln_b[i]])
            x2 = _sgu_out(x2, u, vc, od, g, ln_vecs, sgu_w[i], sgu_b[i].T,
                          cd_w_out[i].astype(BF16))
        x2 = _ffn(x2, g, wg[1], wu[1], wd[1], 4, 5)
    return x2.reshape(bsz, seq, d)
```

```python
import functools

import jax
import jax.numpy as jnp
from jax import lax
from jax.experimental import pallas as pl
from jax.experimental.pallas import tpu as pltpu

HEAD_DIM = 64
MOBA_BLOCK = 256
MOBA_TOPK = 3
CONV_K = 31
SGU_CHUNK = 128
C_GROUPS = 4
ROPE_THETA = 10000.0
RMS_EPS = 1e-6
LN_EPS = 1e-5
LOG2E = 1.4426950408889634

LANES = 128
CONV_HALO = 32
VMEM_LIMIT = 56 * 1024 * 1024

F32 = jnp.float32
BF16 = jnp.bfloat16


def _rms(x, g):
    return x * lax.rsqrt(jnp.mean(x * x, axis=-1, keepdims=True) + RMS_EPS) * g


def _layernorm(x, g, b):
    mu = jnp.mean(x, axis=-1, keepdims=True)
    xc = x - mu
    var = jnp.mean(xc * xc, axis=-1, keepdims=True)
    return xc * lax.rsqrt(var + LN_EPS) * g + b


def _sigmoid(x):
    return 1.0 / (1.0 + jnp.exp(-x))


def _gelu_tanh(x):
    c = 0.7978845608028654
    return 0.5 * x * (1.0 + jnp.tanh(c * (x + 0.044715 * (x * x * x))))


def _resident(arr, lead):
    tail = arr.shape[len(lead):]
    index = tuple(lead) + (0,) * len(tail)
    return pl.BlockSpec((None,) * len(lead) + tail, lambda *_: index,
                        pipeline_mode=pl.Buffered(1))


def _params(n_grid):
    return pltpu.CompilerParams(
        dimension_semantics=("arbitrary",) * n_grid, vmem_limit_bytes=VMEM_LIMIT)


def _ffn_kernel(x_ref, g_ref, wg_ref, wu_ref, wd_ref, o_ref, act_ref, *, g_in, g_out, tf):
    x = x_ref[...]
    h = _rms(x, g_ref[g_in:g_in + 1, :]).astype(BF16)
    d_ff = wg_ref.shape[1]
    for c in range(d_ff // tf):
        sl = slice(c * tf, (c + 1) * tf)
        gate = jnp.dot(h, wg_ref[:, sl], preferred_element_type=F32)
        up = jnp.dot(h, wu_ref[:, sl], preferred_element_type=F32)
        act_ref[:, sl] = (gate * _sigmoid(gate) * up).astype(BF16)
    f = jnp.dot(act_ref[...], wd_ref[...], preferred_element_type=F32)
    o_ref[...] = x + 0.5 * _rms(f, g_ref[g_out:g_out + 1, :])


def _ffn(x2, g, wg, wu, wd, layer, half, tm=512, tf=256):
    t, d = x2.shape
    d_ff = wg.shape[-1]
    g_in, g_out = (0, 1) if half == 0 else (4, 5)
    return pl.pallas_call(
        functools.partial(_ffn_kernel, g_in=g_in, g_out=g_out, tf=tf),
        grid=(t // tm,),
        in_specs=[
            pl.BlockSpec((tm, d), lambda i: (i, 0)),
            _resident(g, (layer,)),
            _resident(wg, (layer, half)),
            _resident(wu, (layer, half)),
            _resident(wd, (layer, half)),
        ],
        out_specs=pl.BlockSpec((tm, d), lambda i: (i, 0)),
        out_shape=jax.ShapeDtypeStruct((t, d), F32),
        scratch_shapes=[pltpu.VMEM((tm, d_ff), BF16)],
        compiler_params=_params(1),
        name="ffn",
    )(x2, g, wg, wu, wd)


def _rope(x, cos, sin_signed):
    half = HEAD_DIM // 2
    lane = lax.broadcasted_iota(jnp.int32, x.shape, 1)
    first_half = (lane % HEAD_DIM) < half
    partner = jnp.where(first_half, pltpu.roll(x, LANES - half, 1), pltpu.roll(x, half, 1))
    return x * cos + partner * sin_signed


def _inproj_kernel(x_ref, g_ref, w_ref, *rest, rope, width):
    if rope:
        cos_ref, sin_ref = rest[:2]
        outs = rest[2:]
    else:
        outs = rest
    h = _rms(x_ref[...], g_ref[2:3, :]).astype(BF16)
    for n, o_ref in enumerate(outs):
        y = jnp.dot(h, w_ref[:, n * width:(n + 1) * width], preferred_element_type=F32)
        if rope and n < 2:
            for t in range(width // LANES):
                sl = slice(t * LANES, (t + 1) * LANES)
                o_ref[:, sl] = _rope(y[:, sl], cos_ref[...], sin_ref[...]).astype(o_ref.dtype)
        else:
            o_ref[...] = y.astype(o_ref.dtype)


def _inproj(x2, g, w, layer, seq, rope_tabs, out_dtypes, tm=512):
    t, d = x2.shape
    width = w.shape[-1] // len(out_dtypes)
    n_seq = seq // tm
    in_specs = [
        pl.BlockSpec((tm, d), lambda i: (i, 0)),
        _resident(g, (layer,)),
        _resident(w, (layer // 2,)),
    ]
    args = [x2, g, w]
    if rope_tabs is not None:
        in_specs += [pl.BlockSpec((tm, LANES), lambda i: (i % n_seq, 0))] * 2
        args += list(rope_tabs)
    return pl.pallas_call(
        functools.partial(_inproj_kernel, rope=rope_tabs is not None, width=width),
        grid=(t // tm,),
        in_specs=in_specs,
        out_specs=[pl.BlockSpec((tm, width), lambda i: (i, 0))] * len(out_dtypes),
        out_shape=[jax.ShapeDtypeStruct((t, width), dt) for dt in out_dtypes],
        compiler_params=_params(1),
        name="inproj_rope" if rope_tabs is not None else "inproj",
    )(*args)


def _moba_select_bias(gate_t, own):
    sub = lax.broadcasted_iota(jnp.int32, gate_t.shape, 0)
    past = sub < own
    gm = jnp.where(past, gate_t, -jnp.inf)
    rank = jnp.zeros(gate_t.shape, jnp.int32)
    for i in range(own):
        gi = gm[i:i + 1, :]
        beats = (gi > gm) | ((gi == gm) & (sub > i))
        rank = rank + jnp.where(beats, 1, 0)
    return jnp.where(past & (rank < MOBA_TOPK), 0.0, -jnp.inf)


def _moba_kernel(q_ref, k_ref, v_ref, o_ref, qm_ref, qs_ref, kb_ref, vt_ref, km_ref):
    seq = q_ref.shape[1]
    blk = MOBA_BLOCK
    n_blk = seq // blk
    n_sub = 8 * pl.cdiv(n_blk, 8)
    scale = HEAD_DIM ** -0.5
    nt = (((1,), (1,)), ((), ()))
    kb_ref[...] = k_ref[0].astype(BF16)
    vt_ref[...] = v_ref[0].T.astype(BF16)
    km_ref[...] = jnp.zeros(km_ref.shape, F32)
    for j in range(n_blk):
        km_ref[j:j + 1, :] = jnp.mean(k_ref[0, j * blk:(j + 1) * blk, :], axis=0, keepdims=True)
    lane = lax.broadcasted_iota(jnp.int32, (1, LANES), 1)
    for hh in range(2):
        qm = jnp.where(lane // HEAD_DIM == hh, q_ref[0], 0.0)
        qm_ref[hh] = qm
        qs_ref[hh] = (qm * scale).astype(BF16)
    key = lax.broadcasted_iota(jnp.int32, (blk, blk), 0)
    qry = lax.broadcasted_iota(jnp.int32, (blk, blk), 1)
    causal_bias = jnp.where(key <= qry, 0.0, -jnp.inf)
    head_rows = lax.broadcasted_iota(jnp.int32, (LANES, 1), 0) < HEAD_DIM

    def scores(i, hh):
        rows = slice(i * blk, (i + 1) * blk)
        s_t = lax.dot_general(kb_ref[0:(i + 1) * blk, :], qs_ref[hh, rows, :], nt,
                              preferred_element_type=F32)
        gate_t = None
        if i > MOBA_TOPK:
            gate_t = lax.dot_general(km_ref[...], qm_ref[hh, rows, :], nt,
                                     precision=lax.Precision.HIGHEST,
                                     preferred_element_type=F32)[0:n_sub]
        return s_t, gate_t

    def softmax(i, s_t, gate_t):
        pieces = [s_t[j * blk:(j + 1) * blk] for j in range(i + 1)]
        if gate_t is not None:
            bias = _moba_select_bias(gate_t, i)
            pieces = [pieces[j] + bias[j:j + 1, :] for j in range(i)] + pieces[i:]
        pieces[i] = pieces[i] + causal_bias
        m = pieces[0]
        for piece in pieces[1:]:
            m = jnp.maximum(m, piece)
        m = jnp.max(m, axis=0, keepdims=True)
        p = jnp.concatenate([jnp.exp(piece - m) for piece in pieces], axis=0)
        return p.astype(BF16), jnp.sum(p, axis=0, keepdims=True)

    def weighted_values(i, p, denom):
        o_t = jnp.dot(vt_ref[:, 0:(i + 1) * blk], p, preferred_element_type=F32)
        return o_t * (1.0 / denom)

    units = [(i, hh) for i in range(n_blk) for hh in range(2)]
    pending = scores(*units[0])
    outs = []
    for u, (i, hh) in enumerate(units):
        current = pending
        if u + 1 < len(units):
            pending = scores(*units[u + 1])
        outs.append(weighted_values(i, *softmax(i, *current)))
        if hh == 1:
            o_t = jnp.where(head_rows, outs[0], outs[1])
            o_ref[0, i * blk:(i + 1) * blk, :] = o_t.T
            outs = []


def _log_one_minus_beta(zn):
    e = jnp.exp2(jnp.abs(zn) * (-LOG2E))
    return jnp.minimum(zn, 0.0) - jnp.log(1.0 + e)


def _suffix_sums(x, tri):
    hi = x.astype(BF16)
    lo = (x - hi.astype(F32)).astype(BF16)
    return (jnp.dot(hi, tri, preferred_element_type=F32)
            + jnp.dot(lo, tri, preferred_element_type=F32))


def _stick_kernel(q_ref, k_ref, v_ref, o_ref, qn_ref, kb_ref, vb_ref, *, tq):
    seq = q_ref.shape[1]
    n_q = seq // tq
    scale = HEAD_DIM ** -0.5
    kb_ref[...] = k_ref[0].astype(BF16)
    vb_ref[...] = v_ref[0].astype(BF16)
    lane = lax.broadcasted_iota(jnp.int32, (1, LANES), 1)
    q_neg = q_ref[0] * (-scale)
    for hh in range(2):
        qn_ref[hh] = jnp.where(lane // HEAD_DIM == hh, q_neg, 0.0).astype(BF16)
    row = lax.broadcasted_iota(jnp.int32, (tq, tq), 0)
    col = lax.broadcasted_iota(jnp.int32, (tq, tq), 1)
    strict = col < row
    tri = (row > col).astype(BF16)

    def both_heads(qns, c, state, diag):
        start = pl.multiple_of(c * tq, tq)
        kc = kb_ref[pl.ds(start, tq), :]
        vc = vb_ref[pl.ds(start, tq), :]
        zns = [lax.dot_general(qn, kc, (((1,), (1,)), ((), ())), preferred_element_type=F32)
               for qn in qns]
        log_1ms = []
        for zn in zns:
            log_1m = _log_one_minus_beta(zn)
            log_1ms.append(jnp.where(strict, log_1m, 0.0) if diag else log_1m)
        afters = [_suffix_sums(log_1m, tri) for log_1m in log_1ms]
        new = []
        for zn, log_1m, after, (acc, carry) in zip(zns, log_1ms, afters, state):
            att = jnp.exp2(((log_1m - zn) + (after + carry)) * LOG2E)
            if diag:
                att = jnp.where(strict, att, 0.0)
            out = jnp.dot(att.astype(BF16), vc, preferred_element_type=F32)
            new.append((acc + out, carry + jnp.sum(log_1m, axis=1, keepdims=True)))
        return tuple(new)

    def q_block(i, _):
        q_start = pl.multiple_of(i * tq, tq)
        qns = [qn_ref[hh, pl.ds(q_start, tq), :] for hh in range(2)]
        zero = (jnp.zeros((tq, LANES), F32), jnp.zeros((tq, 1), F32))
        state = both_heads(qns, i, (zero, zero), True)
        state = lax.fori_loop(
            0, i, lambda step, st: both_heads(qns, i - 1 - step, st, False), state)
        o_ref[0, pl.ds(q_start, tq), :] = jnp.where(lane < HEAD_DIM, state[0][0], state[1][0])
        return 0

    lax.fori_loop(0, n_q, q_block, 0)


def _attention(kind, q, k, v, bsz, seq):
    width = q.shape[1]
    q3, k3, v3 = (a.reshape(bsz, seq, width) for a in (q, k, v))
    spec = pl.BlockSpec((1, seq, LANES), lambda b, p: (b, 0, p))
    if kind == "moba":
        body = _moba_kernel
        scratch = [pltpu.VMEM((2, seq, LANES), F32),
                   pltpu.VMEM((2, seq, LANES), BF16),
                   pltpu.VMEM((seq, LANES), BF16),
                   pltpu.VMEM((LANES, seq), BF16),
                   pltpu.VMEM((LANES, LANES), F32)]
    else:
        body = functools.partial(_stick_kernel, tq=256)
        scratch = [pltpu.VMEM((2, seq, LANES), BF16),
                   pltpu.VMEM((seq, LANES), BF16),
                   pltpu.VMEM((seq, LANES), BF16)]
    out = pl.pallas_call(
        body,
        grid=(bsz, width // LANES),
        in_specs=[spec, spec, spec],
        out_specs=spec,
        out_shape=jax.ShapeDtypeStruct((bsz, seq, width), F32),
        scratch_shapes=scratch,
        compiler_params=_params(2),
        name=kind,
    )(q3, k3, v3)
    return out.reshape(bsz * seq, width)


def _conv_out_kernel(x_ref, oa_ref, ga_ref, gb_ref, hga_ref, hgb_ref, g_ref, cw_ref, cvec_ref,
                     w_ref, o_ref, ext_ref):
    ts = x_ref.shape[0]
    first = pl.program_id(1) == 0
    halo = hga_ref[...] * _sigmoid(hgb_ref[...])
    ext_ref[0:CONV_HALO, :] = jnp.where(first, 0.0, halo)
    ext_ref[CONV_HALO:CONV_HALO + ts, :] = ga_ref[...] * _sigmoid(gb_ref[...])
    off = CONV_HALO - (CONV_K - 1)
    acc = jnp.zeros((ts, ga_ref.shape[1]), F32) + cvec_ref[0:1, :]
    for tap in range(CONV_K):
        acc = acc + ext_ref[off + tap:off + tap + ts, :] * cw_ref[tap:tap + 1, :]
    y = _layernorm(acc, cvec_ref[1:2, :], cvec_ref[2:3, :])
    ob = y * _sigmoid(y)
    wa = oa_ref.shape[1]
    m = (jnp.dot(oa_ref[...].astype(BF16), w_ref[0:wa, :], preferred_element_type=F32)
         + jnp.dot(ob.astype(BF16), w_ref[wa:, :], preferred_element_type=F32))
    o_ref[...] = x_ref[...] + _rms(m, g_ref[3:4, :])


def _conv_out(x2, oa, ga, gb, g, conv_w, conv_vecs, w_out, layer, bsz, seq, ts=512):
    t, d = x2.shape
    par = (layer // 2,)
    wb = ga.shape[1]
    n_seq = seq // ts
    halo_per_tile = ts // CONV_HALO
    tile = lambda width: pl.BlockSpec((ts, width), lambda b, i: (b * n_seq + i, 0))
    halo = pl.BlockSpec(
        (CONV_HALO, wb),
        lambda b, i: (jnp.maximum((b * n_seq + i) * halo_per_tile - 1, 0), 0))
    return pl.pallas_call(
        _conv_out_kernel,
        grid=(bsz, n_seq),
        in_specs=[tile(d), tile(oa.shape[1]), tile(wb), tile(wb), halo, halo,
                  _resident(g, (layer,)), _resident(conv_w, par), _resident(conv_vecs, par),
                  _resident(w_out, par)],
        out_specs=tile(d),
        out_shape=jax.ShapeDtypeStruct((t, d), F32),
        scratch_shapes=[pltpu.VMEM((CONV_HALO + ts, wb), F32)],
        compiler_params=_params(2),
        name="conv_out",
    )(x2, oa, ga, gb, ga, gb, g, conv_w, conv_vecs, w_out)


def _sgu_out_kernel(x_ref, u_ref, vc_ref, od_ref, g_ref, lnv_ref, ws_ref, bs_ref, w_ref, o_ref,
                    oc_ref):
    ts = x_ref.shape[0]
    n_chunk = ts // SGU_CHUNK
    gw = vc_ref.shape[1] // C_GROUPS
    v = _layernorm(_gelu_tanh(vc_ref[...]), lnv_ref[0:1, :], lnv_ref[1:2, :]).astype(BF16)
    row = lax.broadcasted_iota(jnp.int32, (SGU_CHUNK, SGU_CHUNK), 0)
    col = lax.broadcasted_iota(jnp.int32, (SGU_CHUNK, SGU_CHUNK), 1)
    for grp in range(C_GROUPS):
        w_s = jnp.where(col <= row, ws_ref[grp], 0.0).astype(BF16)
        lanes = slice(grp * gw, (grp + 1) * gw)
        vg = jnp.concatenate(
            [v[n * SGU_CHUNK:(n + 1) * SGU_CHUNK, lanes] for n in range(n_chunk)], axis=1)
        mixed = jnp.dot(w_s, vg, preferred_element_type=F32) + bs_ref[:, grp:grp + 1]
        for n in range(n_chunk):
            rows = slice(n * SGU_CHUNK, (n + 1) * SGU_CHUNK)
            oc_ref[rows, lanes] = (_gelu_tanh(u_ref[rows, lanes])
                                   * mixed[:, n * gw:(n + 1) * gw]).astype(BF16)
    wc = u_ref.shape[1]
    m = (jnp.dot(oc_ref[...], w_ref[0:wc, :], preferred_element_type=F32)
         + jnp.dot(od_ref[...].astype(BF16), w_ref[wc:, :], preferred_element_type=F32))
    o_ref[...] = x_ref[...] + _rms(m, g_ref[3:4, :])


def _sgu_out(x2, u, vc, od, g, ln_vecs, w_s, b_s_t, w_out, layer, ts=512):
    t, d = x2.shape
    wc = u.shape[1]
    par = (layer // 2,)
    tile = lambda width: pl.BlockSpec((ts, width), lambda i: (i, 0))
    return pl.pallas_call(
        _sgu_out_kernel,
        grid=(t // ts,),
        in_specs=[tile(d), tile(wc), tile(wc), tile(od.shape[1]),
                  _resident(g, (layer,)), _resident(ln_vecs, par), _resident(w_s, par),
                  _resident(b_s_t, par), _resident(w_out, par)],
        out_specs=tile(d),
        out_shape=jax.ShapeDtypeStruct((t, d), F32),
        scratch_shapes=[pltpu.VMEM((ts, wc), BF16)],
        compiler_params=_params(1),
        name="sgu_out",
    )(x2, u, vc, od, g, ln_vecs, w_s, b_s_t, w_out)


def _rope_tables(seq):
    half = HEAD_DIM // 2
    pos = jnp.arange(seq, dtype=F32)
    inv = ROPE_THETA ** (-jnp.arange(0, HEAD_DIM, 2, dtype=F32) / HEAD_DIM)
    ang = pos[:, None] * inv[None, :]
    cos, sin = jnp.cos(ang), jnp.sin(ang)
    reps = LANES // HEAD_DIM
    cos_t = jnp.tile(jnp.concatenate([cos, cos], axis=1), (1, reps))
    sin_t = jnp.tile(jnp.concatenate([-sin, sin], axis=1), (1, reps))
    return cos_t, sin_t


def kernel(x, norm_g, ffn_w_gate, ffn_w_up, ffn_w_down, ab_w_in, ab_w_out, conv_w, conv_b,
           conv_ln_g, conv_ln_b, cd_w_in, cd_w_out, sgu_ln_g, sgu_ln_b, sgu_w, sgu_b):
    bsz, seq, d = x.shape
    depth = norm_g.shape[0]
    rope_tabs = _rope_tables(seq)
    wg, wu, wd = (w.astype(BF16) for w in (ffn_w_gate, ffn_w_up, ffn_w_down))
    ab_in, ab_out, cd_in, cd_out = (w.astype(BF16) for w in (ab_w_in, ab_w_out, cd_w_in, cd_w_out))
    conv_vecs = jnp.stack([conv_b, conv_ln_g, conv_ln_b], axis=1)
    ln_vecs = jnp.stack([sgu_ln_g, sgu_ln_b], axis=1)
    sgu_b_t = jnp.swapaxes(sgu_b, 1, 2)
    x2 = x.reshape(bsz * seq, d)
    for layer in range(depth):
        x2 = _ffn(x2, norm_g, wg, wu, wd, layer, 0)
        if layer % 2 == 0:
            q, k, v, ga, gb = _inproj(x2, norm_g, ab_in, layer, seq, rope_tabs, [F32] * 5)
            oa = _attention("moba", q, k, v, bsz, seq)
            x2 = _conv_out(x2, oa, ga, gb, norm_g, conv_w, conv_vecs, ab_out, layer, bsz, seq)
        else:
            u, vc, q, k, v = _inproj(x2, norm_g, cd_in, layer, seq, None, [F32] * 5)
            od = _attention("stick", q, k, v, bsz, seq)
            x2 = _sgu_out(x2, u, vc, od, norm_g, ln_vecs, sgu_w, sgu_b_t, cd_out, layer)
        x2 = _ffn(x2, norm_g, wg, wu, wd, layer, 1)
    return x2.reshape(bsz, seq, d)
```

```python
import functools

import jax
import jax.numpy as jnp
import numpy as np
from jax import lax
from jax.experimental import pallas as pl
from jax.experimental.pallas import tpu as pltpu

HEAD_DIM = 64
MOBA_BLOCK = 256
MOBA_TOPK = 3
CONV_K = 31
SGU_CHUNK = 128
C_GROUPS = 4
ROPE_THETA = 10000.0
RMS_EPS = 1e-6
LN_EPS = 1e-5
LOG2E = 1.4426950408889634

LANES = 128
SUBLANES = 8
CONV_HALO = 32
STICK_TILE = 256
STICK_TILES_PER_STEP = 4
STICK_DEAD_BELOW = -110.0
VMEM_LIMIT = 56 * 1024 * 1024

F32 = jnp.float32
BF16 = jnp.bfloat16


def _rms(x, g):
    return x * lax.rsqrt(jnp.mean(x * x, axis=-1, keepdims=True) + RMS_EPS) * g


def _layernorm(x, g, b):
    mu = jnp.mean(x, axis=-1, keepdims=True)
    xc = x - mu
    var = jnp.mean(xc * xc, axis=-1, keepdims=True)
    return xc * lax.rsqrt(var + LN_EPS) * g + b


def _sigmoid(x):
    return 1.0 / (1.0 + jnp.exp(-x))


def _gelu_tanh(x):
    c = 0.7978845608028654
    return 0.5 * x * (1.0 + jnp.tanh(c * (x + 0.044715 * (x * x * x))))


def _resident(arr, lead):
    tail = arr.shape[len(lead):]
    index = tuple(lead) + (0,) * len(tail)
    return pl.BlockSpec((None,) * len(lead) + tail, lambda *_: index,
                        pipeline_mode=pl.Buffered(1))


def _params(n_grid):
    return pltpu.CompilerParams(
        dimension_semantics=("arbitrary",) * n_grid, vmem_limit_bytes=VMEM_LIMIT)


def _ffn_kernel(x_ref, g_ref, wg_ref, wu_ref, wd_ref, o_ref, act_ref, *, g_in, g_out, tf):
    x = x_ref[...]
    h = _rms(x, g_ref[g_in:g_in + 1, :]).astype(BF16)
    d_ff = wg_ref.shape[1]
    for c in range(d_ff // tf):
        sl = slice(c * tf, (c + 1) * tf)
        gate = jnp.dot(h, wg_ref[:, sl], preferred_element_type=F32)
        up = jnp.dot(h, wu_ref[:, sl], preferred_element_type=F32)
        act_ref[:, sl] = (gate * _sigmoid(gate) * up).astype(BF16)
    f = jnp.dot(act_ref[...], wd_ref[...], preferred_element_type=F32)
    o_ref[...] = x + 0.5 * _rms(f, g_ref[g_out:g_out + 1, :])


def _ffn(x2, g, wg, wu, wd, layer, half, tm=512, tf=256):
    t, d = x2.shape
    d_ff = wg.shape[-1]
    g_in, g_out = (0, 1) if half == 0 else (4, 5)
    return pl.pallas_call(
        functools.partial(_ffn_kernel, g_in=g_in, g_out=g_out, tf=tf),
        grid=(t // tm,),
        in_specs=[
            pl.BlockSpec((tm, d), lambda i: (i, 0)),
            _resident(g, (layer,)),
            _resident(wg, (layer, half)),
            _resident(wu, (layer, half)),
            _resident(wd, (layer, half)),
        ],
        out_specs=pl.BlockSpec((tm, d), lambda i: (i, 0)),
        out_shape=jax.ShapeDtypeStruct((t, d), F32),
        scratch_shapes=[pltpu.VMEM((tm, d_ff), BF16)],
        compiler_params=_params(1),
        name="ffn",
    )(x2, g, wg, wu, wd)


def _rope(x, cos, sin_signed):
    half = HEAD_DIM // 2
    lane = lax.broadcasted_iota(jnp.int32, x.shape, 1)
    first_half = (lane % HEAD_DIM) < half
    partner = jnp.where(first_half, pltpu.roll(x, LANES - half, 1), pltpu.roll(x, half, 1))
    return x * cos + partner * sin_signed


def _inproj_kernel(x_ref, g_ref, w_ref, *rest, rope, width):
    if rope:
        cos_ref, sin_ref = rest[:2]
        outs = rest[2:]
    else:
        outs = rest
    h = _rms(x_ref[...], g_ref[2:3, :]).astype(BF16)
    for n, o_ref in enumerate(outs):
        y = jnp.dot(h, w_ref[:, n * width:(n + 1) * width], preferred_element_type=F32)
        if rope and n < 2:
            for t in range(width // LANES):
                sl = slice(t * LANES, (t + 1) * LANES)
                o_ref[:, sl] = _rope(y[:, sl], cos_ref[...], sin_ref[...]).astype(o_ref.dtype)
        else:
            o_ref[...] = y.astype(o_ref.dtype)


def _inproj(x2, g, w, layer, seq, rope_tabs, out_dtypes, tm=512):
    t, d = x2.shape
    width = w.shape[-1] // len(out_dtypes)
    n_seq = seq // tm
    in_specs = [
        pl.BlockSpec((tm, d), lambda i: (i, 0)),
        _resident(g, (layer,)),
        _resident(w, (layer // 2,)),
    ]
    args = [x2, g, w]
    if rope_tabs is not None:
        in_specs += [pl.BlockSpec((tm, LANES), lambda i: (i % n_seq, 0))] * 2
        args += list(rope_tabs)
    return pl.pallas_call(
        functools.partial(_inproj_kernel, rope=rope_tabs is not None, width=width),
        grid=(t // tm,),
        in_specs=in_specs,
        out_specs=[pl.BlockSpec((tm, width), lambda i: (i, 0))] * len(out_dtypes),
        out_shape=[jax.ShapeDtypeStruct((t, width), dt) for dt in out_dtypes],
        compiler_params=_params(1),
        name="inproj_rope" if rope_tabs is not None else "inproj",
    )(*args)


def _moba_select_bias(gate_t, own):
    sub = lax.broadcasted_iota(jnp.int32, gate_t.shape, 0)
    past = sub < own
    gm = jnp.where(past, gate_t, -jnp.inf)
    rank = jnp.zeros(gate_t.shape, jnp.int32)
    for i in range(own):
        gi = gm[i:i + 1, :]
        beats = (gi > gm) | ((gi == gm) & (sub > i))
        rank = rank + jnp.where(beats, 1, 0)
    return jnp.where(past & (rank < MOBA_TOPK), 0.0, -jnp.inf)


def _moba_kernel(q_ref, k_ref, v_ref, o_ref, qm_ref, qs_ref, kb_ref, vt_ref, km_ref):
    seq = q_ref.shape[1]
    blk = MOBA_BLOCK
    n_blk = seq // blk
    n_sub = 8 * pl.cdiv(n_blk, 8)
    scale = HEAD_DIM ** -0.5
    nt = (((1,), (1,)), ((), ()))
    kb_ref[...] = k_ref[0].astype(BF16)
    vt_ref[...] = v_ref[0].T.astype(BF16)
    km_ref[...] = jnp.zeros(km_ref.shape, F32)
    for j in range(n_blk):
        km_ref[j:j + 1, :] = jnp.mean(k_ref[0, j * blk:(j + 1) * blk, :], axis=0, keepdims=True)
    lane = lax.broadcasted_iota(jnp.int32, (1, LANES), 1)
    for hh in range(2):
        qm = jnp.where(lane // HEAD_DIM == hh, q_ref[0], 0.0)
        qm_ref[hh] = qm
        qs_ref[hh] = (qm * scale).astype(BF16)
    key = lax.broadcasted_iota(jnp.int32, (blk, blk), 0)
    qry = lax.broadcasted_iota(jnp.int32, (blk, blk), 1)
    causal_bias = jnp.where(key <= qry, 0.0, -jnp.inf)
    head_rows = lax.broadcasted_iota(jnp.int32, (LANES, 1), 0) < HEAD_DIM

    def scores(i, hh):
        rows = slice(i * blk, (i + 1) * blk)
        s_t = lax.dot_general(kb_ref[0:(i + 1) * blk, :], qs_ref[hh, rows, :], nt,
                              preferred_element_type=F32)
        gate_t = None
        if i > MOBA_TOPK:
            gate_t = lax.dot_general(km_ref[...], qm_ref[hh, rows, :], nt,
                                     precision=lax.Precision.HIGHEST,
                                     preferred_element_type=F32)[0:n_sub]
        return s_t, gate_t

    def softmax(i, s_t, gate_t):
        pieces = [s_t[j * blk:(j + 1) * blk] for j in range(i + 1)]
        if gate_t is not None:
            bias = _moba_select_bias(gate_t, i)
            pieces = [pieces[j] + bias[j:j + 1, :] for j in range(i)] + pieces[i:]
        pieces[i] = pieces[i] + causal_bias
        m = pieces[0]
        for piece in pieces[1:]:
            m = jnp.maximum(m, piece)
        m = jnp.max(m, axis=0, keepdims=True)
        p = jnp.concatenate([jnp.exp(piece - m) for piece in pieces], axis=0)
        return p.astype(BF16), jnp.sum(p, axis=0, keepdims=True)

    def weighted_values(i, p, denom):
        o_t = jnp.dot(vt_ref[:, 0:(i + 1) * blk], p, preferred_element_type=F32)
        return o_t * (1.0 / denom)

    units = [(i, hh) for i in range(n_blk) for hh in range(2)]
    pending = scores(*units[0])
    outs = []
    for u, (i, hh) in enumerate(units):
        current = pending
        if u + 1 < len(units):
            pending = scores(*units[u + 1])
        outs.append(weighted_values(i, *softmax(i, *current)))
        if hh == 1:
            o_t = jnp.where(head_rows, outs[0], outs[1])
            o_ref[0, i * blk:(i + 1) * blk, :] = o_t.T
            outs = []


def _log_one_minus_beta(zn):
    e = jnp.exp2(jnp.abs(zn) * (-LOG2E))
    return jnp.minimum(zn, 0.0) - jnp.log(1.0 + e)


def _suffix_sums(x, tri2):
    hi = x.astype(BF16)
    lo = (x - hi.astype(F32)).astype(BF16)
    return jnp.dot(jnp.concatenate([hi, lo], axis=1), tri2, preferred_element_type=F32)


def _stick_tiles(n_q, n_par):
    entries, first = [], [0]
    for d in range(n_q):
        group = [(i, i - d) for i in range(d, n_q)]
        entries += group + [(n_q, 0)] * (-len(group) % n_par)
        first.append(len(entries))
    return np.array(entries, np.int32).T, np.array(first, np.int32)


def _stick_kernel(tab_ref, first_ref, q_ref, k_ref, v_ref, o_ref, qn_ref, kb_ref, vb_ref,
                  acc_ref, carry_ref, *, tq, n_par):
    seq = q_ref.shape[1]
    n_q = seq // tq
    scale = HEAD_DIM ** -0.5
    nt = (((1,), (1,)), ((), ()))
    kb_ref[...] = k_ref[0].astype(BF16)
    vb_ref[...] = v_ref[0].astype(BF16)
    lane = lax.broadcasted_iota(jnp.int32, (1, LANES), 1)
    q_neg = q_ref[0] * (-scale)
    spare = pl.ds(seq, tq)
    for hh in range(2):
        qn_ref[hh, 0:seq, :] = jnp.where(lane // HEAD_DIM == hh, q_neg, 0.0).astype(BF16)
        qn_ref[hh, spare, :] = jnp.zeros((tq, LANES), BF16)
        acc_ref[hh, spare, :] = jnp.zeros((tq, LANES), F32)
        carry_ref[hh, spare, :] = jnp.zeros((tq, LANES), F32)
    row = lax.broadcasted_iota(jnp.int32, (tq, tq), 0)
    col = lax.broadcasted_iota(jnp.int32, (tq, tq), 1)
    strict = col < row
    tri = (row > col).astype(BF16)
    tri2 = jnp.concatenate([tri, tri], axis=0)

    def process(first, diag):
        chains = []
        for t in range(n_par):
            q_start = pl.multiple_of(tab_ref[0, first + t] * tq, tq)
            k_start = pl.multiple_of(tab_ref[1, first + t] * tq, tq)
            kc = kb_ref[pl.ds(k_start, tq), :]
            vc = vb_ref[pl.ds(k_start, tq), :]
            for hh in range(2):
                rows = (hh, pl.ds(q_start, tq), slice(None))
                zn = lax.dot_general(qn_ref[rows], kc, nt, preferred_element_type=F32)
                chains.append(dict(rows=rows, vc=vc, zn=zn))
        for ch in chains:
            log_1m = _log_one_minus_beta(ch["zn"])
            ch["log_1m"] = jnp.where(strict, log_1m, 0.0) if diag else log_1m
        for ch in chains:
            total = jnp.broadcast_to(jnp.sum(ch["log_1m"], axis=1, keepdims=True), (tq, LANES))
            if diag:
                carry_ref[ch["rows"]] = total
            else:
                ch["carry"] = carry_ref[ch["rows"]]
                carry_ref[ch["rows"]] = ch["carry"] + total
        for ch in chains:
            ch["after"] = _suffix_sums(ch["log_1m"], tri2)
        for ch in chains:
            log_a = (ch["log_1m"] - ch["zn"]) + ch["after"]
            if not diag:
                log_a = log_a + jnp.concatenate([ch["carry"]] * (tq // LANES), axis=1)
            att = jnp.exp2(log_a * LOG2E)
            if diag:
                att = jnp.where(strict, att, 0.0)
            ch["att"] = att.astype(BF16)
        for ch in chains:
            out = jnp.dot(ch["att"], ch["vc"], preferred_element_type=F32)
            if diag:
                acc_ref[ch["rows"]] = out
            else:
                acc_ref[ch["rows"]] += out

    def steps(d, diag):
        first = first_ref[d]

        def step(it, _):
            process(first + it * n_par, diag)
            return 0

        lax.fori_loop(0, (first_ref[d + 1] - first) // n_par, step, 0)

    def more_to_do(state):
        d, live = state
        return (d < n_q) & (live > STICK_DEAD_BELOW)

    def distance(state):
        d, _ = state
        steps(d, False)
        rows = lax.broadcasted_iota(jnp.int32, (seq, LANES), 0)
        carry = jnp.maximum(carry_ref[0, 0:seq, :], carry_ref[1, 0:seq, :])
        live = jnp.max(jnp.where(rows >= (d + 1) * tq, carry, -jnp.inf))
        return d + 1, live

    steps(0, True)
    lax.while_loop(more_to_do, distance, (jnp.int32(1), jnp.float32(0.0)))
    o_ref[0] = jnp.where(lane < HEAD_DIM, acc_ref[0, 0:seq, :], acc_ref[1, 0:seq, :])


def _attention(kind, q, k, v, bsz, seq):
    width = q.shape[1]
    q3, k3, v3 = (a.reshape(bsz, seq, width) for a in (q, k, v))
    spec = pl.BlockSpec((1, seq, LANES), lambda b, p: (b, 0, p))
    if kind == "moba":
        body = _moba_kernel
        scratch = [pltpu.VMEM((2, seq, LANES), F32),
                   pltpu.VMEM((2, seq, LANES), BF16),
                   pltpu.VMEM((seq, LANES), BF16),
                   pltpu.VMEM((LANES, seq), BF16),
                   pltpu.VMEM((LANES, LANES), F32)]
        in_specs, args = [spec, spec, spec], (q3, k3, v3)
    else:
        tq, n_par = STICK_TILE, STICK_TILES_PER_STEP
        tiles, first = _stick_tiles(seq // tq, n_par)
        body = functools.partial(_stick_kernel, tq=tq, n_par=n_par)
        scratch = [pltpu.VMEM((2, seq + tq, LANES), BF16),
                   pltpu.VMEM((seq, LANES), BF16),
                   pltpu.VMEM((seq, LANES), BF16),
                   pltpu.VMEM((2, seq + tq, LANES), F32),
                   pltpu.VMEM((2, seq + tq, LANES), F32)]
        smem = pl.BlockSpec(memory_space=pltpu.SMEM)
        in_specs = [smem, smem, spec, spec, spec]
        args = (jnp.asarray(tiles), jnp.asarray(first), q3, k3, v3)
    out = pl.pallas_call(
        body,
        grid=(bsz, width // LANES),
        in_specs=in_specs,
        out_specs=spec,
        out_shape=jax.ShapeDtypeStruct((bsz, seq, width), F32),
        scratch_shapes=scratch,
        compiler_params=_params(2),
        name=kind,
    )(*args)
    return out.reshape(bsz * seq, width)


def _conv_out_kernel(x_ref, oa_ref, ga_ref, gb_ref, hga_ref, hgb_ref, g_ref, cw_ref, cvec_ref,
                     w_ref, o_ref, ext_ref, win_ref):
    ts = x_ref.shape[0]
    first = pl.program_id(1) == 0
    halo = hga_ref[...] * _sigmoid(hgb_ref[...])
    ext_ref[0:CONV_HALO, :] = jnp.where(first, 0.0, halo)
    ext_ref[CONV_HALO:CONV_HALO + ts, :] = ga_ref[...] * _sigmoid(gb_ref[...])
    off = CONV_HALO - (CONV_K - 1)
    acc = jnp.zeros((ts, ga_ref.shape[1]), F32) + cvec_ref[0:1, :]
    for phase in range(SUBLANES):
        taps = [tap for tap in range(CONV_K) if (off + tap) % SUBLANES == phase]
        if not taps:
            continue
        span = (off + taps[-1]) // SUBLANES * SUBLANES + ts
        win_ref[0:span, :] = ext_ref[phase:phase + span, :]
        for tap in taps:
            start = (off + tap) // SUBLANES * SUBLANES
            acc = acc + win_ref[start:start + ts, :] * cw_ref[tap:tap + 1, :]
    y = _layernorm(acc, cvec_ref[1:2, :], cvec_ref[2:3, :])
    ob = y * _sigmoid(y)
    wa = oa_ref.shape[1]
    m = (jnp.dot(oa_ref[...].astype(BF16), w_ref[0:wa, :], preferred_element_type=F32)
         + jnp.dot(ob.astype(BF16), w_ref[wa:, :], preferred_element_type=F32))
    o_ref[...] = x_ref[...] + _rms(m, g_ref[3:4, :])


def _conv_out(x2, oa, ga, gb, g, conv_w, conv_vecs, w_out, layer, bsz, seq, ts=512):
    t, d = x2.shape
    par = (layer // 2,)
    wb = ga.shape[1]
    n_seq = seq // ts
    halo_per_tile = ts // CONV_HALO
    tile = lambda width: pl.BlockSpec((ts, width), lambda b, i: (b * n_seq + i, 0))
    halo = pl.BlockSpec(
        (CONV_HALO, wb),
        lambda b, i: (jnp.maximum((b * n_seq + i) * halo_per_tile - 1, 0), 0))
    return pl.pallas_call(
        _conv_out_kernel,
        grid=(bsz, n_seq),
        in_specs=[tile(d), tile(oa.shape[1]), tile(wb), tile(wb), halo, halo,
                  _resident(g, (layer,)), _resident(conv_w, par), _resident(conv_vecs, par),
                  _resident(w_out, par)],
        out_specs=tile(d),
        out_shape=jax.ShapeDtypeStruct((t, d), F32),
        scratch_shapes=[pltpu.VMEM((CONV_HALO + ts, wb), F32)] * 2,
        compiler_params=_params(2),
        name="conv_out",
    )(x2, oa, ga, gb, ga, gb, g, conv_w, conv_vecs, w_out)


def _sgu_out_kernel(x_ref, u_ref, vc_ref, od_ref, g_ref, lnv_ref, ws_ref, bs_ref, w_ref, o_ref,
                    oc_ref):
    ts = x_ref.shape[0]
    n_chunk = ts // SGU_CHUNK
    gw = vc_ref.shape[1] // C_GROUPS
    v = _layernorm(_gelu_tanh(vc_ref[...]), lnv_ref[0:1, :], lnv_ref[1:2, :]).astype(BF16)
    row = lax.broadcasted_iota(jnp.int32, (SGU_CHUNK, SGU_CHUNK), 0)
    col = lax.broadcasted_iota(jnp.int32, (SGU_CHUNK, SGU_CHUNK), 1)
    for grp in range(C_GROUPS):
        w_s = jnp.where(col <= row, ws_ref[grp], 0.0).astype(BF16)
        lanes = slice(grp * gw, (grp + 1) * gw)
        vg = jnp.concatenate(
            [v[n * SGU_CHUNK:(n + 1) * SGU_CHUNK, lanes] for n in range(n_chunk)], axis=1)
        mixed = jnp.dot(w_s, vg, preferred_element_type=F32) + bs_ref[:, grp:grp + 1]
        for n in range(n_chunk):
            rows = slice(n * SGU_CHUNK, (n + 1) * SGU_CHUNK)
            oc_ref[rows, lanes] = (_gelu_tanh(u_ref[rows, lanes])
                                   * mixed[:, n * gw:(n + 1) * gw]).astype(BF16)
    wc = u_ref.shape[1]
    m = (jnp.dot(oc_ref[...], w_ref[0:wc, :], preferred_element_type=F32)
         + jnp.dot(od_ref[...].astype(BF16), w_ref[wc:, :], preferred_element_type=F32))
    o_ref[...] = x_ref[...] + _rms(m, g_ref[3:4, :])


def _sgu_out(x2, u, vc, od, g, ln_vecs, w_s, b_s_t, w_out, layer, ts=512):
    t, d = x2.shape
    wc = u.shape[1]
    par = (layer // 2,)
    tile = lambda width: pl.BlockSpec((ts, width), lambda i: (i, 0))
    return pl.pallas_call(
        _sgu_out_kernel,
        grid=(t // ts,),
        in_specs=[tile(d), tile(wc), tile(wc), tile(od.shape[1]),
                  _resident(g, (layer,)), _resident(ln_vecs, par), _resident(w_s, par),
                  _resident(b_s_t, par), _resident(w_out, par)],
        out_specs=tile(d),
        out_shape=jax.ShapeDtypeStruct((t, d), F32),
        scratch_shapes=[pltpu.VMEM((ts, wc), BF16)],
        compiler_params=_params(1),
        name="sgu_out",
    )(x2, u, vc, od, g, ln_vecs, w_s, b_s_t, w_out)


def _rope_tables(seq):
    half = HEAD_DIM // 2
    pos = jnp.arange(seq, dtype=F32)
    inv = ROPE_THETA ** (-jnp.arange(0, HEAD_DIM, 2, dtype=F32) / HEAD_DIM)
    ang = pos[:, None] * inv[None, :]
    cos, sin = jnp.cos(ang), jnp.sin(ang)
    reps = LANES // HEAD_DIM
    cos_t = jnp.tile(jnp.concatenate([cos, cos], axis=1), (1, reps))
    sin_t = jnp.tile(jnp.concatenate([-sin, sin], axis=1), (1, reps))
    return cos_t, sin_t


def kernel(x, norm_g, ffn_w_gate, ffn_w_up, ffn_w_down, ab_w_in, ab_w_out, conv_w, conv_b,
           conv_ln_g, conv_ln_b, cd_w_in, cd_w_out, sgu_ln_g, sgu_ln_b, sgu_w, sgu_b):
    bsz, seq, d = x.shape
    depth = norm_g.shape[0]
    rope_tabs = _rope_tables(seq)
    wg, wu, wd = (w.astype(BF16) for w in (ffn_w_gate, ffn_w_up, ffn_w_down))
    ab_in, ab_out, cd_in, cd_out = (w.astype(BF16) for w in (ab_w_in, ab_w_out, cd_w_in, cd_w_out))
    conv_vecs = jnp.stack([conv_b, conv_ln_g, conv_ln_b], axis=1)
    ln_vecs = jnp.stack([sgu_ln_g, sgu_ln_b], axis=1)
    sgu_b_t = jnp.swapaxes(sgu_b, 1, 2)
    x2 = x.reshape(bsz * seq, d)
    for layer in range(depth):
        x2 = _ffn(x2, norm_g, wg, wu, wd, layer, 0)
        if layer % 2 == 0:
            q, k, v, ga, gb = _inproj(x2, norm_g, ab_in, layer, seq, rope_tabs, [F32] * 5)
            oa = _attention("moba", q, k, v, bsz, seq)
            x2 = _conv_out(x2, oa, ga, gb, norm_g, conv_w, conv_vecs, ab_out, layer, bsz, seq)
        else:
            u, vc, q, k, v = _inproj(x2, norm_g, cd_in, layer, seq, None, [F32] * 5)
            od = _attention("stick", q, k, v, bsz, seq)
            x2 = _sgu_out(x2, u, vc, od, norm_g, ln_vecs, sgu_w, sgu_b_t, cd_out, layer)
        x2 = _ffn(x2, norm_g, wg, wu, wd, layer, 1)
    return x2.reshape(bsz, seq, d)
```

```python
import functools

import jax
import jax.numpy as jnp
import numpy as np
from jax import lax
from jax.experimental import pallas as pl
from jax.experimental.pallas import tpu as pltpu

HEAD_DIM = 64
MOBA_BLOCK = 256
MOBA_TOPK = 3
CONV_K = 31
SGU_CHUNK = 128
C_GROUPS = 4
ROPE_THETA = 10000.0
RMS_EPS = 1e-6
LN_EPS = 1e-5
LOG2E = 1.4426950408889634

LANES = 128
SUBLANES = 8
CONV_HALO = 32
MOBA_ROW_CHUNK = 32
STICK_TILE = 256
STICK_TILES_PER_STEP = 4
STICK_DEAD_BELOW = -110.0
VMEM_LIMIT = 56 * 1024 * 1024

F32 = jnp.float32
BF16 = jnp.bfloat16


def _rms(x, g):
    return x * lax.rsqrt(jnp.mean(x * x, axis=-1, keepdims=True) + RMS_EPS) * g


def _layernorm(x, g, b):
    mu = jnp.mean(x, axis=-1, keepdims=True)
    xc = x - mu
    var = jnp.mean(xc * xc, axis=-1, keepdims=True)
    return xc * lax.rsqrt(var + LN_EPS) * g + b


def _sigmoid(x):
    return 1.0 / (1.0 + jnp.exp(-x))


def _gelu_tanh(x):
    c = 0.7978845608028654
    return 0.5 * x * (1.0 + jnp.tanh(c * (x + 0.044715 * (x * x * x))))


def _resident(arr, lead):
    tail = arr.shape[len(lead):]
    index = tuple(lead) + (0,) * len(tail)
    return pl.BlockSpec((None,) * len(lead) + tail, lambda *_: index,
                        pipeline_mode=pl.Buffered(1))


def _params(n_grid):
    return pltpu.CompilerParams(
        dimension_semantics=("arbitrary",) * n_grid, vmem_limit_bytes=VMEM_LIMIT)


def _ffn_kernel(x_ref, g_ref, wg_ref, wu_ref, wd_ref, o_ref, act_ref, *, g_in, g_out, tf):
    n_sub, ts, d_ff = act_ref.shape

    def rows(s):
        return slice(s * ts, (s + 1) * ts)

    def normed_input(s):
        return _rms(x_ref[rows(s), :], g_ref[g_in:g_in + 1, :]).astype(BF16)

    def finish(s, f):
        o_ref[rows(s), :] = x_ref[rows(s), :] + 0.5 * _rms(f, g_ref[g_out:g_out + 1, :])

    h = normed_input(0)
    f_prev = None
    for s in range(n_sub):
        h_next = None
        for c in range(d_ff // tf):
            sl = slice(c * tf, (c + 1) * tf)
            gate = jnp.dot(h, wg_ref[:, sl], preferred_element_type=F32)
            up = jnp.dot(h, wu_ref[:, sl], preferred_element_type=F32)
            act_ref[s, :, sl] = (gate * _sigmoid(gate) * up).astype(BF16)
            if c == 0:
                if s + 1 < n_sub:
                    h_next = normed_input(s + 1)
                if f_prev is not None:
                    finish(s - 1, f_prev)
        f_prev = jnp.dot(act_ref[s], wd_ref[...], preferred_element_type=F32)
        h = h_next
    finish(n_sub - 1, f_prev)


def _ffn(x2, g, wg, wu, wd, layer, half, tm=1024, n_sub=2, tf=256):
    t, d = x2.shape
    d_ff = wg.shape[-1]
    g_in, g_out = (0, 1) if half == 0 else (4, 5)
    return pl.pallas_call(
        functools.partial(_ffn_kernel, g_in=g_in, g_out=g_out, tf=tf),
        grid=(t // tm,),
        in_specs=[
            pl.BlockSpec((tm, d), lambda i: (i, 0)),
            _resident(g, (layer,)),
            _resident(wg, (layer, half)),
            _resident(wu, (layer, half)),
            _resident(wd, (layer, half)),
        ],
        out_specs=pl.BlockSpec((tm, d), lambda i: (i, 0)),
        out_shape=jax.ShapeDtypeStruct((t, d), F32),
        scratch_shapes=[pltpu.VMEM((n_sub, tm // n_sub, d_ff), BF16)],
        compiler_params=_params(1),
        name="ffn",
    )(x2, g, wg, wu, wd)


def _rope(x, cos, sin_signed):
    half = HEAD_DIM // 2
    lane = lax.broadcasted_iota(jnp.int32, x.shape, 1)
    first_half = (lane % HEAD_DIM) < half
    partner = jnp.where(first_half, pltpu.roll(x, LANES - half, 1), pltpu.roll(x, half, 1))
    return x * cos + partner * sin_signed


def _inproj_kernel(x_ref, g_ref, w_ref, *rest, rope, width):
    if rope:
        cos_ref, sin_ref = rest[:2]
        outs = rest[2:]
    else:
        outs = rest
    h = _rms(x_ref[...], g_ref[2:3, :]).astype(BF16)
    for n, o_ref in enumerate(outs):
        y = jnp.dot(h, w_ref[:, n * width:(n + 1) * width], preferred_element_type=F32)
        if rope and n < 2:
            for t in range(width // LANES):
                sl = slice(t * LANES, (t + 1) * LANES)
                o_ref[:, sl] = _rope(y[:, sl], cos_ref[...], sin_ref[...]).astype(o_ref.dtype)
        else:
            o_ref[...] = y.astype(o_ref.dtype)


def _inproj(x2, g, w, layer, seq, rope_tabs, out_dtypes, tm=512):
    t, d = x2.shape
    width = w.shape[-1] // len(out_dtypes)
    n_seq = seq // tm
    in_specs = [
        pl.BlockSpec((tm, d), lambda i: (i, 0)),
        _resident(g, (layer,)),
        _resident(w, (layer // 2,)),
    ]
    args = [x2, g, w]
    if rope_tabs is not None:
        in_specs += [pl.BlockSpec((tm, LANES), lambda i: (i % n_seq, 0))] * 2
        args += list(rope_tabs)
    return pl.pallas_call(
        functools.partial(_inproj_kernel, rope=rope_tabs is not None, width=width),
        grid=(t // tm,),
        in_specs=in_specs,
        out_specs=[pl.BlockSpec((tm, width), lambda i: (i, 0))] * len(out_dtypes),
        out_shape=[jax.ShapeDtypeStruct((t, width), dt) for dt in out_dtypes],
        compiler_params=_params(1),
        name="inproj_rope" if rope_tabs is not None else "inproj",
    )(*args)


def _moba_select_bias(gate_t, own):
    sub = lax.broadcasted_iota(jnp.int32, gate_t.shape, 0)
    past = sub < own
    gm = jnp.where(past, gate_t, -jnp.inf)
    rank = jnp.zeros(gate_t.shape, jnp.int32)
    for i in range(own):
        gi = gm[i:i + 1, :]
        beats = (gi > gm) | ((gi == gm) & (sub > i))
        rank = rank + jnp.where(beats, 1, 0)
    return jnp.where(past & (rank < MOBA_TOPK), 0.0, -jnp.inf)


def _moba_kernel(q_ref, k_ref, v_ref, o_ref, qm_ref, qs_ref, kb_ref, vt_ref, km_ref, s_ref,
                 p_ref):
    seq = q_ref.shape[1]
    blk = MOBA_BLOCK
    n_blk = seq // blk
    n_sub = 8 * pl.cdiv(n_blk, 8)
    scale = HEAD_DIM ** -0.5
    nt = (((1,), (1,)), ((), ()))
    kb_ref[...] = k_ref[0].astype(BF16)
    vt_ref[0:LANES, :] = v_ref[0].astype(F32).T.astype(BF16)
    vt_ref[LANES:, :] = jnp.ones((2 * SUBLANES, seq), BF16)
    km_ref[...] = jnp.zeros(km_ref.shape, F32)
    for j in range(n_blk):
        km_ref[j:j + 1, :] = jnp.mean(k_ref[0, j * blk:(j + 1) * blk, :], axis=0, keepdims=True)
    lane = lax.broadcasted_iota(jnp.int32, (1, LANES), 1)
    for hh in range(2):
        qm = jnp.where(lane // HEAD_DIM == hh, q_ref[0], 0.0)
        qm_ref[hh] = qm
        qs_ref[hh] = (qm * (scale * LOG2E)).astype(BF16)
    key = lax.broadcasted_iota(jnp.int32, (blk, blk), 0)
    qry = lax.broadcasted_iota(jnp.int32, (blk, blk), 1)
    causal_bias = jnp.where(key <= qry, 0.0, -jnp.inf)
    head_rows = lax.broadcasted_iota(jnp.int32, (LANES, 1), 0) < HEAD_DIM

    def scores(i, hh):
        rows = slice(i * blk, (i + 1) * blk)
        n_keys = (i + 1) * blk
        s_ref[2 * (i % 2) + hh, 0:n_keys, :] = lax.dot_general(
            kb_ref[0:n_keys, :], qs_ref[hh, rows, :], nt, preferred_element_type=F32)
        if i <= MOBA_TOPK:
            return None
        return lax.dot_general(km_ref[...], qm_ref[hh, rows, :], nt,
                               precision=lax.Precision.HIGHEST,
                               preferred_element_type=F32)[0:n_sub]

    def softmax(i, hh, gate_t):
        buf = 2 * (i % 2) + hh
        biases = [None] * i + [causal_bias]
        if gate_t is not None:
            bias = _moba_select_bias(gate_t, i)
            biases = [bias[j:j + 1, :] for j in range(i)] + [causal_bias]

        def chunks(j):
            for r in range(0, blk, MOBA_ROW_CHUNK):
                yield r, slice(j * blk + r, j * blk + r + MOBA_ROW_CHUNK)

        m = None
        for j, b in enumerate(biases):
            m_j = None
            for r, rows in chunks(j):
                x = s_ref[buf, rows, :]
                if j == i:
                    x = x + causal_bias[r:r + MOBA_ROW_CHUNK]
                m_j = x if m_j is None else jnp.maximum(m_j, x)
            m_j = jnp.max(m_j, axis=0, keepdims=True)
            if j < i and b is not None:
                m_j = m_j + b
            m = m_j if m is None else jnp.maximum(m, m_j)
        for j, b in enumerate(biases):
            for r, rows in chunks(j):
                if j == i:
                    x = s_ref[buf, rows, :] + (causal_bias[r:r + MOBA_ROW_CHUNK] - m)
                else:
                    x = s_ref[buf, rows, :] - (m if b is None else m - b)
                p_ref[hh, rows, :] = jnp.exp2(x).astype(BF16)

    def weighted_values(i, hh):
        n_keys = (i + 1) * blk
        o_t = jnp.dot(vt_ref[:, 0:n_keys], p_ref[hh, 0:n_keys, :], preferred_element_type=F32)
        return o_t[0:LANES] * (1.0 / o_t[LANES:LANES + 1])

    pending = [scores(0, hh) for hh in range(2)]
    for i in range(n_blk):
        gates = pending
        if i + 1 < n_blk:
            pending = [scores(i + 1, hh) for hh in range(2)]
        for hh in range(2):
            softmax(i, hh, gates[hh])
        outs = [weighted_values(i, hh) for hh in range(2)]
        o_t = jnp.where(head_rows, outs[0], outs[1])
        o_ref[0, i * blk:(i + 1) * blk, :] = o_t.T.astype(o_ref.dtype)


def _log_one_minus_beta(zn):
    e = jnp.exp2(jnp.abs(zn) * (-LOG2E))
    return jnp.minimum(zn, 0.0) - jnp.log(1.0 + e)


def _suffix_sums(x, tri2):
    hi = x.astype(BF16)
    lo = (x - hi.astype(F32)).astype(BF16)
    return jnp.dot(jnp.concatenate([hi, lo], axis=1), tri2, preferred_element_type=F32)


def _stick_tiles(n_q, n_par):
    entries, first = [], [0]
    for d in range(n_q):
        group = [(i, i - d) for i in range(d, n_q)]
        entries += group + [(n_q, 0)] * (-len(group) % n_par)
        first.append(len(entries))
    return np.array(entries, np.int32).T, np.array(first, np.int32)


def _stick_kernel(tab_ref, first_ref, q_ref, k_ref, v_ref, o_ref, qn_ref, acc_ref, carry_ref,
                  *, tq, n_par):
    seq = q_ref.shape[1]
    n_q = seq // tq
    scale = HEAD_DIM ** -0.5
    nt = (((1,), (1,)), ((), ()))
    lane = lax.broadcasted_iota(jnp.int32, (1, LANES), 1)
    q_neg = q_ref[0].astype(F32) * (-scale)
    spare = pl.ds(seq, tq)
    for hh in range(2):
        qn_ref[hh, 0:seq, :] = jnp.where(lane // HEAD_DIM == hh, q_neg, 0.0).astype(BF16)
        qn_ref[hh, spare, :] = jnp.zeros((tq, LANES), BF16)
        acc_ref[hh, spare, :] = jnp.zeros((tq, LANES), F32)
        carry_ref[hh, spare, :] = jnp.zeros((tq, LANES), F32)
    row = lax.broadcasted_iota(jnp.int32, (tq, tq), 0)
    col = lax.broadcasted_iota(jnp.int32, (tq, tq), 1)
    strict = col < row
    tri = (row > col).astype(BF16)
    tri2 = jnp.concatenate([tri, tri], axis=0)

    def process(first, diag):
        chains = []
        for t in range(n_par):
            q_start = pl.multiple_of(tab_ref[0, first + t] * tq, tq)
            k_start = pl.multiple_of(tab_ref[1, first + t] * tq, tq)
            kc = k_ref[0, pl.ds(k_start, tq), :]
            vc = v_ref[0, pl.ds(k_start, tq), :]
            for hh in range(2):
                rows = (hh, pl.ds(q_start, tq), slice(None))
                zn = lax.dot_general(qn_ref[rows], kc, nt, preferred_element_type=F32)
                chains.append(dict(rows=rows, vc=vc, zn=zn))
        for ch in chains:
            log_1m = _log_one_minus_beta(ch["zn"])
            ch["log_1m"] = jnp.where(strict, log_1m, 0.0) if diag else log_1m
        for ch in chains:
            total = jnp.broadcast_to(jnp.sum(ch["log_1m"], axis=1, keepdims=True), (tq, LANES))
            if diag:
                carry_ref[ch["rows"]] = total
            else:
                ch["carry"] = carry_ref[ch["rows"]]
                carry_ref[ch["rows"]] = ch["carry"] + total
        for ch in chains:
            ch["after"] = _suffix_sums(ch["log_1m"], tri2)
        for ch in chains:
            log_a = (ch["log_1m"] - ch["zn"]) + ch["after"]
            if not diag:
                log_a = log_a + jnp.concatenate([ch["carry"]] * (tq // LANES), axis=1)
            att = jnp.exp2(log_a * LOG2E)
            if diag:
                att = jnp.where(strict, att, 0.0)
            ch["att"] = att.astype(BF16)
        for ch in chains:
            out = jnp.dot(ch["att"], ch["vc"], preferred_element_type=F32)
            if diag:
                acc_ref[ch["rows"]] = out
            else:
                acc_ref[ch["rows"]] += out

    def steps(d, diag):
        first = first_ref[d]

        def step(it, _):
            process(first + it * n_par, diag)
            return 0

        lax.fori_loop(0, (first_ref[d + 1] - first) // n_par, step, 0)

    def more_to_do(state):
        d, live = state
        return (d < n_q) & (live > STICK_DEAD_BELOW)

    def distance(state):
        d, _ = state
        steps(d, False)
        rows = lax.broadcasted_iota(jnp.int32, (seq, LANES), 0)
        carry = jnp.maximum(carry_ref[0, 0:seq, :], carry_ref[1, 0:seq, :])
        live = jnp.max(jnp.where(rows >= (d + 1) * tq, carry, -jnp.inf))
        return d + 1, live

    steps(0, True)
    lax.while_loop(more_to_do, distance, (jnp.int32(1), jnp.float32(0.0)))
    o_ref[0] = jnp.where(lane < HEAD_DIM, acc_ref[0, 0:seq, :],
                         acc_ref[1, 0:seq, :]).astype(o_ref.dtype)


def _attention(kind, q, k, v, bsz, seq):
    width = q.shape[1]
    q3, k3, v3 = (a.reshape(bsz, seq, width) for a in (q, k, v))
    spec = pl.BlockSpec((1, seq, LANES), lambda b, p: (b, 0, p))
    if kind == "moba":
        body = _moba_kernel
        scratch = [pltpu.VMEM((2, seq, LANES), F32),
                   pltpu.VMEM((2, seq, LANES), BF16),
                   pltpu.VMEM((seq, LANES), BF16),
                   pltpu.VMEM((LANES + 2 * SUBLANES, seq), BF16),
                   pltpu.VMEM((LANES, LANES), F32),
                   pltpu.VMEM((4, seq, MOBA_BLOCK), F32),
                   pltpu.VMEM((2, seq, MOBA_BLOCK), BF16)]
        in_specs, args = [spec, spec, spec], (q3, k3, v3)
    else:
        tq, n_par = STICK_TILE, STICK_TILES_PER_STEP
        tiles, first = _stick_tiles(seq // tq, n_par)
        body = functools.partial(_stick_kernel, tq=tq, n_par=n_par)
        assert q.dtype == k.dtype == v.dtype == BF16
        scratch = [pltpu.VMEM((2, seq + tq, LANES), BF16),
                   pltpu.VMEM((2, seq + tq, LANES), F32),
                   pltpu.VMEM((2, seq + tq, LANES), F32)]
        smem = pl.BlockSpec(memory_space=pltpu.SMEM)
        in_specs = [smem, smem, spec, spec, spec]
        args = (jnp.asarray(tiles), jnp.asarray(first), q3, k3, v3)
    out = pl.pallas_call(
        body,
        grid=(bsz, width // LANES),
        in_specs=in_specs,
        out_specs=spec,
        out_shape=jax.ShapeDtypeStruct((bsz, seq, width), BF16),
        scratch_shapes=scratch,
        compiler_params=_params(2),
        name=kind,
    )(*args)
    return out.reshape(bsz * seq, width)


def _conv_out_kernel(x_ref, oa_ref, ga_ref, gb_ref, hga_ref, hgb_ref, g_ref, cw_ref, cvec_ref,
                     w_ref, o_ref, ext_ref, win_ref):
    ts = x_ref.shape[0]
    first = pl.program_id(1) == 0
    halo = hga_ref[...].astype(F32) * _sigmoid(hgb_ref[...].astype(F32))
    ext_ref[0:CONV_HALO, :] = jnp.where(first, 0.0, halo)
    ext_ref[CONV_HALO:CONV_HALO + ts, :] = (ga_ref[...].astype(F32)
                                            * _sigmoid(gb_ref[...].astype(F32)))
    off = CONV_HALO - (CONV_K - 1)
    acc = jnp.zeros((ts, ga_ref.shape[1]), F32) + cvec_ref[0:1, :]
    for phase in range(SUBLANES):
        taps = [tap for tap in range(CONV_K) if (off + tap) % SUBLANES == phase]
        if not taps:
            continue
        span = (off + taps[-1]) // SUBLANES * SUBLANES + ts
        win_ref[0:span, :] = ext_ref[phase:phase + span, :]
        for tap in taps:
            start = (off + tap) // SUBLANES * SUBLANES
            acc = acc + win_ref[start:start + ts, :] * cw_ref[tap:tap + 1, :]
    y = _layernorm(acc, cvec_ref[1:2, :], cvec_ref[2:3, :])
    ob = y * _sigmoid(y)
    wa = oa_ref.shape[1]
    m = (jnp.dot(oa_ref[...], w_ref[0:wa, :], preferred_element_type=F32)
         + jnp.dot(ob.astype(BF16), w_ref[wa:, :], preferred_element_type=F32))
    o_ref[...] = x_ref[...] + _rms(m, g_ref[3:4, :])


def _conv_out(x2, oa, ga, gb, g, conv_w, conv_vecs, w_out, layer, bsz, seq, ts=512):
    t, d = x2.shape
    par = (layer // 2,)
    wb = ga.shape[1]
    n_seq = seq // ts
    halo_per_tile = ts // CONV_HALO
    tile = lambda width: pl.BlockSpec((ts, width), lambda b, i: (b * n_seq + i, 0))
    halo = pl.BlockSpec(
        (CONV_HALO, wb),
        lambda b, i: (jnp.maximum((b * n_seq + i) * halo_per_tile - 1, 0), 0))
    return pl.pallas_call(
        _conv_out_kernel,
        grid=(bsz, n_seq),
        in_specs=[tile(d), tile(oa.shape[1]), tile(wb), tile(wb), halo, halo,
                  _resident(g, (layer,)), _resident(conv_w, par), _resident(conv_vecs, par),
                  _resident(w_out, par)],
        out_specs=tile(d),
        out_shape=jax.ShapeDtypeStruct((t, d), F32),
        scratch_shapes=[pltpu.VMEM((CONV_HALO + ts, wb), F32)] * 2,
        compiler_params=_params(2),
        name="conv_out",
    )(x2, oa, ga, gb, ga, gb, g, conv_w, conv_vecs, w_out)


def _sgu_out_kernel(x_ref, u_ref, vc_ref, od_ref, g_ref, lnv_ref, ws_ref, bs_ref, w_ref, o_ref,
                    oc_ref):
    ts = x_ref.shape[0]
    n_chunk = ts // SGU_CHUNK
    gw = vc_ref.shape[1] // C_GROUPS
    v = _layernorm(_gelu_tanh(vc_ref[...].astype(F32)), lnv_ref[0:1, :],
                   lnv_ref[1:2, :]).astype(BF16)
    row = lax.broadcasted_iota(jnp.int32, (SGU_CHUNK, SGU_CHUNK), 0)
    col = lax.broadcasted_iota(jnp.int32, (SGU_CHUNK, SGU_CHUNK), 1)
    for grp in range(C_GROUPS):
        w_s = jnp.where(col <= row, ws_ref[grp], 0.0).astype(BF16)
        lanes = slice(grp * gw, (grp + 1) * gw)
        vg = jnp.concatenate(
            [v[n * SGU_CHUNK:(n + 1) * SGU_CHUNK, lanes] for n in range(n_chunk)], axis=1)
        mixed = jnp.dot(w_s, vg, preferred_element_type=F32) + bs_ref[:, grp:grp + 1]
        for n in range(n_chunk):
            rows = slice(n * SGU_CHUNK, (n + 1) * SGU_CHUNK)
            oc_ref[rows, lanes] = (_gelu_tanh(u_ref[rows, lanes].astype(F32))
                                   * mixed[:, n * gw:(n + 1) * gw]).astype(BF16)
    wc = u_ref.shape[1]
    m = (jnp.dot(oc_ref[...], w_ref[0:wc, :], preferred_element_type=F32)
         + jnp.dot(od_ref[...], w_ref[wc:, :], preferred_element_type=F32))
    o_ref[...] = x_ref[...] + _rms(m, g_ref[3:4, :])


def _sgu_out(x2, u, vc, od, g, ln_vecs, w_s, b_s_t, w_out, layer, ts=512):
    t, d = x2.shape
    wc = u.shape[1]
    par = (layer // 2,)
    tile = lambda width: pl.BlockSpec((ts, width), lambda i: (i, 0))
    return pl.pallas_call(
        _sgu_out_kernel,
        grid=(t // ts,),
        in_specs=[tile(d), tile(wc), tile(wc), tile(od.shape[1]),
                  _resident(g, (layer,)), _resident(ln_vecs, par), _resident(w_s, par),
                  _resident(b_s_t, par), _resident(w_out, par)],
        out_specs=tile(d),
        out_shape=jax.ShapeDtypeStruct((t, d), F32),
        scratch_shapes=[pltpu.VMEM((ts, wc), BF16)],
        compiler_params=_params(1),
        name="sgu_out",
    )(x2, u, vc, od, g, ln_vecs, w_s, b_s_t, w_out)


def _rope_tables(seq):
    half = HEAD_DIM // 2
    pos = jnp.arange(seq, dtype=F32)
    inv = ROPE_THETA ** (-jnp.arange(0, HEAD_DIM, 2, dtype=F32) / HEAD_DIM)
    ang = pos[:, None] * inv[None, :]
    cos, sin = jnp.cos(ang), jnp.sin(ang)
    reps = LANES // HEAD_DIM
    cos_t = jnp.tile(jnp.concatenate([cos, cos], axis=1), (1, reps))
    sin_t = jnp.tile(jnp.concatenate([-sin, sin], axis=1), (1, reps))
    return cos_t, sin_t


def kernel(x, norm_g, ffn_w_gate, ffn_w_up, ffn_w_down, ab_w_in, ab_w_out, conv_w, conv_b,
           conv_ln_g, conv_ln_b, cd_w_in, cd_w_out, sgu_ln_g, sgu_ln_b, sgu_w, sgu_b):
    bsz, seq, d = x.shape
    depth = norm_g.shape[0]
    rope_tabs = _rope_tables(seq)
    wg, wu, wd = (w.astype(BF16) for w in (ffn_w_gate, ffn_w_up, ffn_w_down))
    ab_in, ab_out, cd_in, cd_out = (w.astype(BF16) for w in (ab_w_in, ab_w_out, cd_w_in, cd_w_out))
    conv_vecs = jnp.stack([conv_b, conv_ln_g, conv_ln_b], axis=1)
    ln_vecs = jnp.stack([sgu_ln_g, sgu_ln_b], axis=1)
    sgu_b_t = jnp.swapaxes(sgu_b, 1, 2)
    x2 = x.reshape(bsz * seq, d)
    for layer in range(depth):
        x2 = _ffn(x2, norm_g, wg, wu, wd, layer, 0)
        if layer % 2 == 0:
            q, k, v, ga, gb = _inproj(x2, norm_g, ab_in, layer, seq, rope_tabs,
                                      [F32, F32, BF16, BF16, BF16])
            oa = _attention("moba", q, k, v, bsz, seq)
            x2 = _conv_out(x2, oa, ga, gb, norm_g, conv_w, conv_vecs, ab_out, layer, bsz, seq)
        else:
            u, vc, q, k, v = _inproj(x2, norm_g, cd_in, layer, seq, None, [BF16] * 5)
            od = _attention("stick", q, k, v, bsz, seq)
            x2 = _sgu_out(x2, u, vc, od, norm_g, ln_vecs, sgu_w, sgu_b_t, cd_out, layer)
        x2 = _ffn(x2, norm_g, wg, wu, wd, layer, 1)
    return x2.reshape(bsz, seq, d)
```

```python
import functools

import jax
import jax.numpy as jnp
import numpy as np
from jax import lax
from jax.experimental import pallas as pl
from jax.experimental.pallas import tpu as pltpu

HEAD_DIM = 64
MOBA_BLOCK = 256
MOBA_TOPK = 3
CONV_K = 31
SGU_CHUNK = 128
C_GROUPS = 4
ROPE_THETA = 10000.0
RMS_EPS = 1e-6
LN_EPS = 1e-5
LOG2E = 1.4426950408889634

LANES = 128
SUBLANES = 8
CONV_HALO = 32
MOBA_ROW_CHUNK = 32
STICK_TILE = 256
STICK_TILES_PER_STEP = 4
STICK_DEAD_BELOW = -110.0
VMEM_LIMIT = 56 * 1024 * 1024

F32 = jnp.float32
BF16 = jnp.bfloat16


def _rms(x, g):
    return x * lax.rsqrt(jnp.mean(x * x, axis=-1, keepdims=True) + RMS_EPS) * g


def _layernorm(x, g, b):
    mu = jnp.mean(x, axis=-1, keepdims=True)
    xc = x - mu
    var = jnp.mean(xc * xc, axis=-1, keepdims=True)
    return xc * lax.rsqrt(var + LN_EPS) * g + b


def _sigmoid(x):
    return 1.0 / (1.0 + jnp.exp(-x))


def _gelu_tanh(x):
    c = 0.7978845608028654
    return 0.5 * x * (1.0 + jnp.tanh(c * (x + 0.044715 * (x * x * x))))


def _resident(arr, lead):
    tail = arr.shape[len(lead):]
    index = tuple(lead) + (0,) * len(tail)
    return pl.BlockSpec((None,) * len(lead) + tail, lambda *_: index,
                        pipeline_mode=pl.Buffered(1))


def _params(n_grid):
    return pltpu.CompilerParams(
        dimension_semantics=("arbitrary",) * n_grid, vmem_limit_bytes=VMEM_LIMIT)


def _ffn_first_kernel(x_ref, g_ref, wg_ref, wu_ref, wd_ref, o_ref, wgb_ref, wub_ref, wdb_ref,
                      h_ref, acc_ref, *, g_in, g_out):
    c = pl.program_id(0)

    @pl.when(c == 0)
    def _():
        h_ref[...] = _rms(x_ref[...], g_ref[g_in:g_in + 1, :]).astype(BF16)
        acc_ref[...] = jnp.zeros(acc_ref.shape, F32)

    wg, wu, wd = (w[...].astype(BF16) for w in (wg_ref, wu_ref, wd_ref))
    wgb_ref[...], wub_ref[...], wdb_ref[...] = wg, wu, wd
    gate = jnp.dot(h_ref[...], wg, preferred_element_type=F32)
    up = jnp.dot(h_ref[...], wu, preferred_element_type=F32)
    act = (gate * _sigmoid(gate) * up).astype(BF16)
    acc_ref[...] += jnp.dot(act, wd, preferred_element_type=F32)

    @pl.when(c == pl.num_programs(0) - 1)
    def _():
        o_ref[...] = x_ref[...] + 0.5 * _rms(acc_ref[...], g_ref[g_out:g_out + 1, :])


def _ffn_kernel(x_ref, first_ref, g_ref, wg_ref, wu_ref, wd_ref, o_ref, act_ref, *, g_in,
                g_out, tf):
    n_sub, ts, d_ff = act_ref.shape

    def rows(s):
        return slice(s * ts, (s + 1) * ts)

    def normed_input(s):
        return _rms(x_ref[rows(s), :], g_ref[g_in:g_in + 1, :]).astype(BF16)

    def finish(s, f):
        o_ref[rows(s), :] = x_ref[rows(s), :] + 0.5 * _rms(f, g_ref[g_out:g_out + 1, :])

    @pl.when(pl.program_id(0) == 0)
    def _():
        o_ref[...] = first_ref[...]

    @pl.when(pl.program_id(0) > 0)
    def _():
        h = normed_input(0)
        f_prev = None
        for s in range(n_sub):
            h_next = None
            for c in range(d_ff // tf):
                sl = slice(c * tf, (c + 1) * tf)
                gate = jnp.dot(h, wg_ref[:, sl], preferred_element_type=F32)
                up = jnp.dot(h, wu_ref[:, sl], preferred_element_type=F32)
                act_ref[s, :, sl] = (gate * _sigmoid(gate) * up).astype(BF16)
                if c == 0:
                    if s + 1 < n_sub:
                        h_next = normed_input(s + 1)
                    if f_prev is not None:
                        finish(s - 1, f_prev)
            f_prev = jnp.dot(act_ref[s], wd_ref[...], preferred_element_type=F32)
            h = h_next
        finish(n_sub - 1, f_prev)


def _ffn(x2, g, wg, wu, wd, layer, half, tm=1024, n_sub=2, tf=256):
    t, d = x2.shape
    d_ff = wg.shape[-1]
    g_in, g_out = (0, 1) if half == 0 else (4, 5)
    once = pl.Buffered(1)
    col_chunk = lambda c: (layer, half, 0, c)
    row_chunk = lambda c: (layer, half, c, 0)
    first, wgb, wub, wdb = pl.pallas_call(
        functools.partial(_ffn_first_kernel, g_in=g_in, g_out=g_out),
        grid=(d_ff // tf,),
        in_specs=[
            pl.BlockSpec((tm, d), lambda c: (0, 0), pipeline_mode=once),
            _resident(g, (layer,)),
            pl.BlockSpec((None, None, d, tf), col_chunk),
            pl.BlockSpec((None, None, d, tf), col_chunk),
            pl.BlockSpec((None, None, tf, d), row_chunk),
        ],
        out_specs=[
            pl.BlockSpec((tm, d), lambda c: (0, 0)),
            pl.BlockSpec((d, tf), lambda c: (0, c)),
            pl.BlockSpec((d, tf), lambda c: (0, c)),
            pl.BlockSpec((tf, d), lambda c: (c, 0)),
        ],
        out_shape=[
            jax.ShapeDtypeStruct((tm, d), F32),
            jax.ShapeDtypeStruct((d, d_ff), BF16),
            jax.ShapeDtypeStruct((d, d_ff), BF16),
            jax.ShapeDtypeStruct((d_ff, d), BF16),
        ],
        scratch_shapes=[pltpu.VMEM((tm, d), BF16), pltpu.VMEM((tm, d), F32)],
        compiler_params=_params(1),
        name="ffn_first",
    )(x2, g, wg, wu, wd)
    return pl.pallas_call(
        functools.partial(_ffn_kernel, g_in=g_in, g_out=g_out, tf=tf),
        grid=(t // tm,),
        in_specs=[
            pl.BlockSpec((tm, d), lambda i: (i, 0)),
            _resident(first, ()),
            _resident(g, (layer,)),
            _resident(wgb, ()),
            _resident(wub, ()),
            _resident(wdb, ()),
        ],
        out_specs=pl.BlockSpec((tm, d), lambda i: (i, 0)),
        out_shape=jax.ShapeDtypeStruct((t, d), F32),
        scratch_shapes=[pltpu.VMEM((n_sub, tm // n_sub, d_ff), BF16)],
        compiler_params=_params(1),
        name="ffn",
    )(x2, first, g, wgb, wub, wdb)


def _rope(x, cos, sin_signed):
    half = HEAD_DIM // 2
    lane = lax.broadcasted_iota(jnp.int32, x.shape, 1)
    first_half = (lane % HEAD_DIM) < half
    partner = jnp.where(first_half, pltpu.roll(x, LANES - half, 1), pltpu.roll(x, half, 1))
    return x * cos + partner * sin_signed


def _inproj_kernel(x_ref, g_ref, w_ref, *outs, width):
    h = _rms(x_ref[...], g_ref[2:3, :]).astype(BF16)
    for n, o_ref in enumerate(outs):
        y = jnp.dot(h, w_ref[:, n * width:(n + 1) * width], preferred_element_type=F32)
        o_ref[...] = y.astype(o_ref.dtype)


def _inproj(x2, g, w, layer, out_dtypes, tm=1024):
    t, d = x2.shape
    width = w.shape[-1] // len(out_dtypes)
    return pl.pallas_call(
        functools.partial(_inproj_kernel, width=width),
        grid=(t // tm,),
        in_specs=[
            pl.BlockSpec((tm, d), lambda i: (i, 0)),
            _resident(g, (layer,)),
            _resident(w, (layer // 2,)),
        ],
        out_specs=[pl.BlockSpec((tm, width), lambda i: (i, 0))] * len(out_dtypes),
        out_shape=[jax.ShapeDtypeStruct((t, width), dt) for dt in out_dtypes],
        compiler_params=_params(1),
        name="inproj",
    )(x2, g, w)


def _causal_conv(ext_ref, win_ref, cw_ref, bias, ts):
    off = CONV_HALO - (CONV_K - 1)
    acc = jnp.zeros((ts, ext_ref.shape[1]), F32) + bias
    for phase in range(SUBLANES):
        taps = [tap for tap in range(CONV_K) if (off + tap) % SUBLANES == phase]
        span = (off + taps[-1]) // SUBLANES * SUBLANES + ts
        win_ref[0:span, :] = ext_ref[phase:phase + span, :]
        for tap in taps:
            start = (off + tap) // SUBLANES * SUBLANES
            acc = acc + win_ref[start:start + ts, :] * cw_ref[tap:tap + 1, :]
    return acc


def _inproj_conv_kernel(x_ref, g_ref, w_ref, cos_ref, sin_ref, cw_ref, cvec_ref,
                        q_ref, k_ref, v_ref, ob_ref, ext_ref, win_ref, *, steps_per_seq):
    tm, width = v_ref.shape
    h = _rms(x_ref[...], g_ref[2:3, :]).astype(BF16)
    proj = lambda n: jnp.dot(h, w_ref[:, n * width:(n + 1) * width],
                             preferred_element_type=F32)

    @pl.when(pl.program_id(0) % steps_per_seq == 0)
    def _():
        ext_ref[0:CONV_HALO, :] = jnp.zeros((CONV_HALO, width), F32)

    ext_ref[CONV_HALO:CONV_HALO + tm, :] = proj(3) * _sigmoid(proj(4))
    conv = _causal_conv(ext_ref, win_ref, cw_ref, cvec_ref[0:1, :], tm)
    ext_ref[0:CONV_HALO, :] = ext_ref[tm:tm + CONV_HALO, :]
    y = _layernorm(conv, cvec_ref[1:2, :], cvec_ref[2:3, :])
    ob_ref[...] = (y * _sigmoid(y)).astype(ob_ref.dtype)
    for n, o_ref in enumerate((q_ref, k_ref)):
        y = proj(n)
        for t in range(width // LANES):
            sl = slice(t * LANES, (t + 1) * LANES)
            o_ref[:, sl] = _rope(y[:, sl], cos_ref[...], sin_ref[...]).astype(o_ref.dtype)
    v_ref[...] = proj(2).astype(v_ref.dtype)


def _inproj_conv(x2, g, w, conv_w, conv_vecs, layer, seq, rope_tabs, tm=512):
    t, d = x2.shape
    width = w.shape[-1] // 5
    n_seq = seq // tm
    par = (layer // 2,)
    tab = pl.BlockSpec((tm, LANES), lambda i: (i % n_seq, 0))
    out_dtypes = [F32, F32, BF16, BF16]
    return pl.pallas_call(
        functools.partial(_inproj_conv_kernel, steps_per_seq=n_seq),
        grid=(t // tm,),
        in_specs=[
            pl.BlockSpec((tm, d), lambda i: (i, 0)),
            _resident(g, (layer,)),
            _resident(w, par),
            tab, tab,
            _resident(conv_w, par),
            _resident(conv_vecs, par),
        ],
        out_specs=[pl.BlockSpec((tm, width), lambda i: (i, 0))] * 4,
        out_shape=[jax.ShapeDtypeStruct((t, width), dt) for dt in out_dtypes],
        scratch_shapes=[pltpu.VMEM((CONV_HALO + tm, width), F32)] * 2,
        compiler_params=_params(1),
        name="inproj_conv",
    )(x2, g, w, *rope_tabs, conv_w, conv_vecs)


def _moba_select_bias(gate_t, own):
    sub = lax.broadcasted_iota(jnp.int32, gate_t.shape, 0)
    past = sub < own
    gm = jnp.where(past, gate_t, -jnp.inf)
    rank = jnp.zeros(gate_t.shape, jnp.int32)
    for i in range(own):
        gi = gm[i:i + 1, :]
        beats = (gi > gm) | ((gi == gm) & (sub > i))
        rank = rank + jnp.where(beats, 1, 0)
    return jnp.where(past & (rank < MOBA_TOPK), 0.0, -jnp.inf)


def _moba_kernel(q_ref, k_ref, v_ref, o_ref, qm_ref, qs_ref, kb_ref, vt_ref, km_ref, s_ref,
                 p_ref):
    seq = q_ref.shape[1]
    blk = MOBA_BLOCK
    n_blk = seq // blk
    n_sub = 8 * pl.cdiv(n_blk, 8)
    scale = HEAD_DIM ** -0.5
    nt = (((1,), (1,)), ((), ()))
    kb_ref[...] = k_ref[0].astype(BF16)
    vt_ref[0:LANES, :] = v_ref[0].astype(F32).T.astype(BF16)
    vt_ref[LANES:, :] = jnp.ones((2 * SUBLANES, seq), BF16)
    km_ref[...] = jnp.zeros(km_ref.shape, F32)
    for j in range(n_blk):
        km_ref[j:j + 1, :] = jnp.mean(k_ref[0, j * blk:(j + 1) * blk, :], axis=0, keepdims=True)
    lane = lax.broadcasted_iota(jnp.int32, (1, LANES), 1)
    for hh in range(2):
        qm = jnp.where(lane // HEAD_DIM == hh, q_ref[0], 0.0)
        qm_ref[hh] = qm
        qs_ref[hh] = (qm * (scale * LOG2E)).astype(BF16)
    key = lax.broadcasted_iota(jnp.int32, (blk, blk), 0)
    qry = lax.broadcasted_iota(jnp.int32, (blk, blk), 1)
    causal_bias = jnp.where(key <= qry, 0.0, -jnp.inf)
    head_rows = lax.broadcasted_iota(jnp.int32, (LANES, 1), 0) < HEAD_DIM

    def scores(i, hh):
        rows = slice(i * blk, (i + 1) * blk)
        n_keys = (i + 1) * blk
        s_ref[2 * (i % 2) + hh, 0:n_keys, :] = lax.dot_general(
            kb_ref[0:n_keys, :], qs_ref[hh, rows, :], nt, preferred_element_type=F32)
        if i <= MOBA_TOPK:
            return None
        return lax.dot_general(km_ref[...], qm_ref[hh, rows, :], nt,
                               precision=lax.Precision.HIGHEST,
                               preferred_element_type=F32)[0:n_sub]

    def softmax(i, hh, gate_t):
        buf = 2 * (i % 2) + hh
        biases = [None] * i + [causal_bias]
        if gate_t is not None:
            bias = _moba_select_bias(gate_t, i)
            biases = [bias[j:j + 1, :] for j in range(i)] + [causal_bias]

        def chunks(j):
            for r in range(0, blk, MOBA_ROW_CHUNK):
                yield r, slice(j * blk + r, j * blk + r + MOBA_ROW_CHUNK)

        m = None
        for j, b in enumerate(biases):
            m_j = None
            for r, rows in chunks(j):
                x = s_ref[buf, rows, :]
                if j == i:
                    x = x + causal_bias[r:r + MOBA_ROW_CHUNK]
                m_j = x if m_j is None else jnp.maximum(m_j, x)
            m_j = jnp.max(m_j, axis=0, keepdims=True)
            if j < i and b is not None:
                m_j = m_j + b
            m = m_j if m is None else jnp.maximum(m, m_j)
        for j, b in enumerate(biases):
            for r, rows in chunks(j):
                if j == i:
                    x = s_ref[buf, rows, :] + (causal_bias[r:r + MOBA_ROW_CHUNK] - m)
                else:
                    x = s_ref[buf, rows, :] - (m if b is None else m - b)
                p_ref[hh, rows, :] = jnp.exp2(x).astype(BF16)

    def weighted_values(i, hh):
        n_keys = (i + 1) * blk
        o_t = jnp.dot(vt_ref[:, 0:n_keys], p_ref[hh, 0:n_keys, :], preferred_element_type=F32)
        return o_t[0:LANES] * (1.0 / o_t[LANES:LANES + 1])

    pending = [scores(0, hh) for hh in range(2)]
    for i in range(n_blk):
        gates = pending
        if i + 1 < n_blk:
            pending = [scores(i + 1, hh) for hh in range(2)]
        for hh in range(2):
            softmax(i, hh, gates[hh])
        outs = [weighted_values(i, hh) for hh in range(2)]
        o_t = jnp.where(head_rows, outs[0], outs[1])
        o_ref[0, i * blk:(i + 1) * blk, :] = o_t.T.astype(o_ref.dtype)


def _log_one_minus_beta(zn):
    e = jnp.exp2(jnp.abs(zn) * (-LOG2E))
    return jnp.minimum(zn, 0.0) - jnp.log(1.0 + e)


def _suffix_sums(x, tri2):
    hi = x.astype(BF16)
    lo = (x - hi.astype(F32)).astype(BF16)
    return jnp.dot(jnp.concatenate([hi, lo], axis=1), tri2, preferred_element_type=F32)


def _stick_tiles(n_q, n_par):
    entries, first = [], [0]
    for d in range(n_q):
        group = [(i, i - d) for i in range(d, n_q)]
        entries += group + [(n_q, 0)] * (-len(group) % n_par)
        first.append(len(entries))
    return np.array(entries, np.int32).T, np.array(first, np.int32)


def _stick_kernel(tab_ref, first_ref, q_ref, k_ref, v_ref, o_ref, qn_ref, acc_ref, carry_ref,
                  *, tq, n_par):
    seq = q_ref.shape[1]
    n_q = seq // tq
    scale = HEAD_DIM ** -0.5
    nt = (((1,), (1,)), ((), ()))
    lane = lax.broadcasted_iota(jnp.int32, (1, LANES), 1)
    q_neg = q_ref[0].astype(F32) * (-scale)
    spare = pl.ds(seq, tq)
    for hh in range(2):
        qn_ref[hh, 0:seq, :] = jnp.where(lane // HEAD_DIM == hh, q_neg, 0.0).astype(BF16)
        qn_ref[hh, spare, :] = jnp.zeros((tq, LANES), BF16)
        acc_ref[hh, spare, :] = jnp.zeros((tq, LANES), F32)
        carry_ref[hh, spare, :] = jnp.zeros((tq, LANES), F32)
    row = lax.broadcasted_iota(jnp.int32, (tq, tq), 0)
    col = lax.broadcasted_iota(jnp.int32, (tq, tq), 1)
    strict = col < row
    tri = (row > col).astype(BF16)
    tri2 = jnp.concatenate([tri, tri], axis=0)

    def process(first, diag):
        chains = []
        for t in range(n_par):
            q_start = pl.multiple_of(tab_ref[0, first + t] * tq, tq)
            k_start = pl.multiple_of(tab_ref[1, first + t] * tq, tq)
            kc = k_ref[0, pl.ds(k_start, tq), :]
            vc = v_ref[0, pl.ds(k_start, tq), :]
            for hh in range(2):
                rows = (hh, pl.ds(q_start, tq), slice(None))
                zn = lax.dot_general(qn_ref[rows], kc, nt, preferred_element_type=F32)
                chains.append(dict(rows=rows, vc=vc, zn=zn))
        for ch in chains:
            log_1m = _log_one_minus_beta(ch["zn"])
            ch["log_1m"] = jnp.where(strict, log_1m, 0.0) if diag else log_1m
        for ch in chains:
            total = jnp.broadcast_to(jnp.sum(ch["log_1m"], axis=1, keepdims=True), (tq, LANES))
            if diag:
                carry_ref[ch["rows"]] = total
            else:
                ch["carry"] = carry_ref[ch["rows"]]
                carry_ref[ch["rows"]] = ch["carry"] + total
        for ch in chains:
            ch["after"] = _suffix_sums(ch["log_1m"], tri2)
        for ch in chains:
            log_a = (ch["log_1m"] - ch["zn"]) + ch["after"]
            if not diag:
                log_a = log_a + jnp.concatenate([ch["carry"]] * (tq // LANES), axis=1)
            att = jnp.exp2(log_a * LOG2E)
            if diag:
                att = jnp.where(strict, att, 0.0)
            ch["att"] = att.astype(BF16)
        for ch in chains:
            out = jnp.dot(ch["att"], ch["vc"], preferred_element_type=F32)
            if diag:
                acc_ref[ch["rows"]] = out
            else:
                acc_ref[ch["rows"]] += out

    def steps(d, diag):
        first = first_ref[d]

        def step(it, _):
            process(first + it * n_par, diag)
            return 0

        lax.fori_loop(0, (first_ref[d + 1] - first) // n_par, step, 0)

    def more_to_do(state):
        d, live = state
        return (d < n_q) & (live > STICK_DEAD_BELOW)

    def distance(state):
        d, _ = state
        steps(d, False)
        rows = lax.broadcasted_iota(jnp.int32, (seq, LANES), 0)
        carry = jnp.maximum(carry_ref[0, 0:seq, :], carry_ref[1, 0:seq, :])
        live = jnp.max(jnp.where(rows >= (d + 1) * tq, carry, -jnp.inf))
        return d + 1, live

    steps(0, True)
    lax.while_loop(more_to_do, distance, (jnp.int32(1), jnp.float32(0.0)))
    o_ref[0] = jnp.where(lane < HEAD_DIM, acc_ref[0, 0:seq, :],
                         acc_ref[1, 0:seq, :]).astype(o_ref.dtype)


def _attention(kind, q, k, v, bsz, seq):
    width = q.shape[1]
    q3, k3, v3 = (a.reshape(bsz, seq, width) for a in (q, k, v))
    spec = pl.BlockSpec((1, seq, LANES), lambda b, p: (b, 0, p))
    if kind == "moba":
        body = _moba_kernel
        scratch = [pltpu.VMEM((2, seq, LANES), F32),
                   pltpu.VMEM((2, seq, LANES), BF16),
                   pltpu.VMEM((seq, LANES), BF16),
                   pltpu.VMEM((LANES + 2 * SUBLANES, seq), BF16),
                   pltpu.VMEM((LANES, LANES), F32),
                   pltpu.VMEM((4, seq, MOBA_BLOCK), F32),
                   pltpu.VMEM((2, seq, MOBA_BLOCK), BF16)]
        in_specs, args = [spec, spec, spec], (q3, k3, v3)
    else:
        tq, n_par = STICK_TILE, STICK_TILES_PER_STEP
        tiles, first = _stick_tiles(seq // tq, n_par)
        body = functools.partial(_stick_kernel, tq=tq, n_par=n_par)
        assert q.dtype == k.dtype == v.dtype == BF16
        scratch = [pltpu.VMEM((2, seq + tq, LANES), BF16),
                   pltpu.VMEM((2, seq + tq, LANES), F32),
                   pltpu.VMEM((2, seq + tq, LANES), F32)]
        smem = pl.BlockSpec(memory_space=pltpu.SMEM)
        in_specs = [smem, smem, spec, spec, spec]
        args = (jnp.asarray(tiles), jnp.asarray(first), q3, k3, v3)
    out = pl.pallas_call(
        body,
        grid=(bsz, width // LANES),
        in_specs=in_specs,
        out_specs=spec,
        out_shape=jax.ShapeDtypeStruct((bsz, seq, width), BF16),
        scratch_shapes=scratch,
        compiler_params=_params(2),
        name=kind,
    )(*args)
    return out.reshape(bsz * seq, width)


def _out_kernel(x_ref, oa_ref, ob_ref, g_ref, w_ref, o_ref):
    wa = oa_ref.shape[1]
    m = (jnp.dot(oa_ref[...], w_ref[0:wa, :], preferred_element_type=F32)
         + jnp.dot(ob_ref[...], w_ref[wa:, :], preferred_element_type=F32))
    o_ref[...] = x_ref[...] + _rms(m, g_ref[3:4, :])


def _out(x2, oa, ob, g, w_out, layer, ts=1024):
    t, d = x2.shape
    tile = lambda width: pl.BlockSpec((ts, width), lambda i: (i, 0))
    return pl.pallas_call(
        _out_kernel,
        grid=(t // ts,),
        in_specs=[tile(d), tile(oa.shape[1]), tile(ob.shape[1]),
                  _resident(g, (layer,)), _resident(w_out, (layer // 2,))],
        out_specs=tile(d),
        out_shape=jax.ShapeDtypeStruct((t, d), F32),
        compiler_params=_params(1),
        name="out",
    )(x2, oa, ob, g, w_out)


def _sgu_out_kernel(x_ref, u_ref, vc_ref, od_ref, g_ref, lnv_ref, ws_ref, bs_ref, w_ref, o_ref,
                    oc_ref):
    ts = x_ref.shape[0]
    n_chunk = ts // SGU_CHUNK
    gw = vc_ref.shape[1] // C_GROUPS
    v = _layernorm(_gelu_tanh(vc_ref[...].astype(F32)), lnv_ref[0:1, :],
                   lnv_ref[1:2, :]).astype(BF16)
    row = lax.broadcasted_iota(jnp.int32, (SGU_CHUNK, SGU_CHUNK), 0)
    col = lax.broadcasted_iota(jnp.int32, (SGU_CHUNK, SGU_CHUNK), 1)
    for grp in range(C_GROUPS):
        w_s = jnp.where(col <= row, ws_ref[grp], 0.0).astype(BF16)
        lanes = slice(grp * gw, (grp + 1) * gw)
        vg = jnp.concatenate(
            [v[n * SGU_CHUNK:(n + 1) * SGU_CHUNK, lanes] for n in range(n_chunk)], axis=1)
        mixed = jnp.dot(w_s, vg, preferred_element_type=F32) + bs_ref[:, grp:grp + 1]
        for n in range(n_chunk):
            rows = slice(n * SGU_CHUNK, (n + 1) * SGU_CHUNK)
            oc_ref[rows, lanes] = (_gelu_tanh(u_ref[rows, lanes].astype(F32))
                                   * mixed[:, n * gw:(n + 1) * gw]).astype(BF16)
    wc = u_ref.shape[1]
    m = (jnp.dot(oc_ref[...], w_ref[0:wc, :], preferred_element_type=F32)
         + jnp.dot(od_ref[...], w_ref[wc:, :], preferred_element_type=F32))
    o_ref[...] = x_ref[...] + _rms(m, g_ref[3:4, :])


def _sgu_out(x2, u, vc, od, g, ln_vecs, w_s, b_s_t, w_out, layer, ts=1024):
    t, d = x2.shape
    wc = u.shape[1]
    par = (layer // 2,)
    tile = lambda width: pl.BlockSpec((ts, width), lambda i: (i, 0))
    return pl.pallas_call(
        _sgu_out_kernel,
        grid=(t // ts,),
        in_specs=[tile(d), tile(wc), tile(wc), tile(od.shape[1]),
                  _resident(g, (layer,)), _resident(ln_vecs, par), _resident(w_s, par),
                  _resident(b_s_t, par), _resident(w_out, par)],
        out_specs=tile(d),
        out_shape=jax.ShapeDtypeStruct((t, d), F32),
        scratch_shapes=[pltpu.VMEM((ts, wc), BF16)],
        compiler_params=_params(1),
        name="sgu_out",
    )(x2, u, vc, od, g, ln_vecs, w_s, b_s_t, w_out)


def _rope_tables(seq):
    half = HEAD_DIM // 2
    pos = jnp.arange(seq, dtype=F32)
    inv = ROPE_THETA ** (-jnp.arange(0, HEAD_DIM, 2, dtype=F32) / HEAD_DIM)
    ang = pos[:, None] * inv[None, :]
    cos, sin = jnp.cos(ang), jnp.sin(ang)
    reps = LANES // HEAD_DIM
    cos_t = jnp.tile(jnp.concatenate([cos, cos], axis=1), (1, reps))
    sin_t = jnp.tile(jnp.concatenate([-sin, sin], axis=1), (1, reps))
    return cos_t, sin_t


def kernel(x, norm_g, ffn_w_gate, ffn_w_up, ffn_w_down, ab_w_in, ab_w_out, conv_w, conv_b,
           conv_ln_g, conv_ln_b, cd_w_in, cd_w_out, sgu_ln_g, sgu_ln_b, sgu_w, sgu_b):
    bsz, seq, d = x.shape
    depth = norm_g.shape[0]
    rope_tabs = _rope_tables(seq)
    wg, wu, wd = ffn_w_gate, ffn_w_up, ffn_w_down
    ab_in, ab_out, cd_in, cd_out = (w.astype(BF16) for w in (ab_w_in, ab_w_out, cd_w_in, cd_w_out))
    conv_vecs = jnp.stack([conv_b, conv_ln_g, conv_ln_b], axis=1)
    ln_vecs = jnp.stack([sgu_ln_g, sgu_ln_b], axis=1)
    sgu_b_t = jnp.swapaxes(sgu_b, 1, 2)
    x2 = x.reshape(bsz * seq, d)
    for layer in range(depth):
        x2 = _ffn(x2, norm_g, wg, wu, wd, layer, 0)
        if layer % 2 == 0:
            q, k, v, ob = _inproj_conv(x2, norm_g, ab_in, conv_w, conv_vecs, layer, seq,
                                       rope_tabs)
            oa = _attention("moba", q, k, v, bsz, seq)
            x2 = _out(x2, oa, ob, norm_g, ab_out, layer)
        else:
            u, vc, q, k, v = _inproj(x2, norm_g, cd_in, layer, [BF16] * 5)
            od = _attention("stick", q, k, v, bsz, seq)
            x2 = _sgu_out(x2, u, vc, od, norm_g, ln_vecs, sgu_w, sgu_b_t, cd_out, layer)
        x2 = _ffn(x2, norm_g, wg, wu, wd, layer, 1)
    return x2.reshape(bsz, seq, d)
```

```python
import functools

import jax
import jax.numpy as jnp
import numpy as np
from jax import lax
from jax.experimental import pallas as pl
from jax.experimental.pallas import tpu as pltpu

HEAD_DIM = 64
MOBA_BLOCK = 256
MOBA_TOPK = 3
CONV_K = 31
SGU_CHUNK = 128
C_GROUPS = 4
ROPE_THETA = 10000.0
RMS_EPS = 1e-6
LN_EPS = 1e-5
LOG2E = 1.4426950408889634

LANES = 128
SUBLANES = 8
CONV_HALO = 32
MOBA_ROW_CHUNK = 32
STICK_TILE = 256
STICK_TILES_PER_STEP = 8
STICK_DEAD_BELOW = -110.0
VMEM_LIMIT = 56 * 1024 * 1024

F32 = jnp.float32
BF16 = jnp.bfloat16


def _rms(x, g):
    return x * lax.rsqrt(jnp.mean(x * x, axis=-1, keepdims=True) + RMS_EPS) * g


def _layernorm(x, g, b):
    mu = jnp.mean(x, axis=-1, keepdims=True)
    xc = x - mu
    var = jnp.mean(xc * xc, axis=-1, keepdims=True)
    return xc * lax.rsqrt(var + LN_EPS) * g + b


def _sigmoid(x):
    return 1.0 / (1.0 + jnp.exp(-x))


def _gelu_tanh(x):
    c = 0.7978845608028654
    return 0.5 * x * (1.0 + jnp.tanh(c * (x + 0.044715 * (x * x * x))))


def _resident(arr, lead):
    tail = arr.shape[len(lead):]
    index = tuple(lead) + (0,) * len(tail)
    return pl.BlockSpec((None,) * len(lead) + tail, lambda *_: index,
                        pipeline_mode=pl.Buffered(1))


def _params(n_grid):
    return pltpu.CompilerParams(
        dimension_semantics=("arbitrary",) * n_grid, vmem_limit_bytes=VMEM_LIMIT)


def _ffn_kernel(x_ref, g_ref, wg_ref, wu_ref, wd_ref, o_ref, act_ref, *, g_in, g_out, tf):
    n_sub, ts, d_ff = act_ref.shape

    def rows(s):
        return slice(s * ts, (s + 1) * ts)

    def normed_input(s):
        return _rms(x_ref[rows(s), :], g_ref[g_in:g_in + 1, :]).astype(BF16)

    def finish(s, f):
        o_ref[rows(s), :] = x_ref[rows(s), :] + 0.5 * _rms(f, g_ref[g_out:g_out + 1, :])

    h = normed_input(0)
    f_prev = None
    for s in range(n_sub):
        h_next = None
        for c in range(d_ff // tf):
            sl = slice(c * tf, (c + 1) * tf)
            gate = jnp.dot(h, wg_ref[:, sl], preferred_element_type=F32)
            up = jnp.dot(h, wu_ref[:, sl], preferred_element_type=F32)
            act_ref[s, :, sl] = (gate * _sigmoid(gate) * up).astype(BF16)
            if c == 0:
                if s + 1 < n_sub:
                    h_next = normed_input(s + 1)
                if f_prev is not None:
                    finish(s - 1, f_prev)
        f_prev = jnp.dot(act_ref[s], wd_ref[...], preferred_element_type=F32)
        h = h_next
    finish(n_sub - 1, f_prev)


def _ffn(x2, g, wg, wu, wd, layer, half, tm=1024, n_sub=2, tf=256):
    t, d = x2.shape
    d_ff = wg.shape[-1]
    g_in, g_out = (0, 1) if half == 0 else (4, 5)
    return pl.pallas_call(
        functools.partial(_ffn_kernel, g_in=g_in, g_out=g_out, tf=tf),
        grid=(t // tm,),
        in_specs=[
            pl.BlockSpec((tm, d), lambda i: (i, 0)),
            _resident(g, (layer,)),
            _resident(wg, (layer, half)),
            _resident(wu, (layer, half)),
            _resident(wd, (layer, half)),
        ],
        out_specs=pl.BlockSpec((tm, d), lambda i: (i, 0)),
        out_shape=jax.ShapeDtypeStruct((t, d), F32),
        scratch_shapes=[pltpu.VMEM((n_sub, tm // n_sub, d_ff), BF16)],
        compiler_params=_params(1),
        name="ffn",
    )(x2, g, wg, wu, wd)


def _rope(x, cos, sin_signed):
    half = HEAD_DIM // 2
    lane = lax.broadcasted_iota(jnp.int32, x.shape, 1)
    first_half = (lane % HEAD_DIM) < half
    partner = jnp.where(first_half, pltpu.roll(x, LANES - half, 1), pltpu.roll(x, half, 1))
    return x * cos + partner * sin_signed


def _inproj_kernel(x_ref, g_ref, w_ref, *rest, rope, width):
    if rope:
        cos_ref, sin_ref = rest[:2]
        outs = rest[2:]
    else:
        outs = rest
    h = _rms(x_ref[...], g_ref[2:3, :]).astype(BF16)
    for n, o_ref in enumerate(outs):
        y = jnp.dot(h, w_ref[:, n * width:(n + 1) * width], preferred_element_type=F32)
        if rope and n < 2:
            for t in range(width // LANES):
                sl = slice(t * LANES, (t + 1) * LANES)
                o_ref[:, sl] = _rope(y[:, sl], cos_ref[...], sin_ref[...]).astype(o_ref.dtype)
        else:
            o_ref[...] = y.astype(o_ref.dtype)


def _inproj(x2, g, w, layer, seq, rope_tabs, out_dtypes, tm=1024):
    t, d = x2.shape
    width = w.shape[-1] // len(out_dtypes)
    n_seq = seq // tm
    in_specs = [
        pl.BlockSpec((tm, d), lambda i: (i, 0)),
        _resident(g, (layer,)),
        _resident(w, (layer // 2,)),
    ]
    args = [x2, g, w]
    if rope_tabs is not None:
        in_specs += [pl.BlockSpec((tm, LANES), lambda i: (i % n_seq, 0))] * 2
        args += list(rope_tabs)
    return pl.pallas_call(
        functools.partial(_inproj_kernel, rope=rope_tabs is not None, width=width),
        grid=(t // tm,),
        in_specs=in_specs,
        out_specs=[pl.BlockSpec((tm, width), lambda i: (i, 0))] * len(out_dtypes),
        out_shape=[jax.ShapeDtypeStruct((t, width), dt) for dt in out_dtypes],
        compiler_params=_params(1),
        name="inproj_rope" if rope_tabs is not None else "inproj",
    )(*args)


def _moba_select_bias(gate_t, own):
    sub = lax.broadcasted_iota(jnp.int32, gate_t.shape, 0)
    past = sub < own
    gm = jnp.where(past, gate_t, -jnp.inf)
    rank = jnp.zeros(gate_t.shape, jnp.int32)
    for i in range(own):
        gi = gm[i:i + 1, :]
        beats = (gi > gm) | ((gi == gm) & (sub > i))
        rank = rank + jnp.where(beats, 1, 0)
    return jnp.where(past & (rank < MOBA_TOPK), 0.0, -jnp.inf)


def _moba_kernel(q_ref, k_ref, v_ref, o_ref, qm_ref, qs_ref, kb_ref, vt_ref, km_ref, s_ref,
                 p_ref):
    seq = q_ref.shape[1]
    blk = MOBA_BLOCK
    n_blk = seq // blk
    n_sub = 8 * pl.cdiv(n_blk, 8)
    scale = HEAD_DIM ** -0.5
    nt = (((1,), (1,)), ((), ()))
    kb_ref[...] = k_ref[0].astype(BF16)
    vt_ref[0:LANES, :] = v_ref[0].astype(F32).T.astype(BF16)
    vt_ref[LANES:, :] = jnp.ones((2 * SUBLANES, seq), BF16)
    km_ref[...] = jnp.zeros(km_ref.shape, F32)
    for j in range(n_blk):
        km_ref[j:j + 1, :] = jnp.mean(k_ref[0, j * blk:(j + 1) * blk, :], axis=0, keepdims=True)
    lane = lax.broadcasted_iota(jnp.int32, (1, LANES), 1)
    for hh in range(2):
        qm = jnp.where(lane // HEAD_DIM == hh, q_ref[0], 0.0)
        qm_ref[hh] = qm
        qs_ref[hh] = (qm * (scale * LOG2E)).astype(BF16)
    key = lax.broadcasted_iota(jnp.int32, (blk, blk), 0)
    qry = lax.broadcasted_iota(jnp.int32, (blk, blk), 1)
    causal_bias = jnp.where(key <= qry, 0.0, -jnp.inf)
    head_rows = lax.broadcasted_iota(jnp.int32, (LANES, 1), 0) < HEAD_DIM

    def scores(i, hh):
        rows = slice(i * blk, (i + 1) * blk)
        n_keys = (i + 1) * blk
        s_ref[2 * (i % 2) + hh, 0:n_keys, :] = lax.dot_general(
            kb_ref[0:n_keys, :], qs_ref[hh, rows, :], nt, preferred_element_type=F32)
        if i <= MOBA_TOPK:
            return None
        return lax.dot_general(km_ref[...], qm_ref[hh, rows, :], nt,
                               precision=lax.Precision.HIGHEST,
                               preferred_element_type=F32)[0:n_sub]

    def softmax(i, hh, gate_t):
        buf = 2 * (i % 2) + hh
        biases = [None] * i + [causal_bias]
        if gate_t is not None:
            bias = _moba_select_bias(gate_t, i)
            biases = [bias[j:j + 1, :] for j in range(i)] + [causal_bias]

        def chunks(j):
            for r in range(0, blk, MOBA_ROW_CHUNK):
                yield r, slice(j * blk + r, j * blk + r + MOBA_ROW_CHUNK)

        m = None
        for j, b in enumerate(biases):
            m_j = None
            for r, rows in chunks(j):
                x = s_ref[buf, rows, :]
                if j == i:
                    x = x + causal_bias[r:r + MOBA_ROW_CHUNK]
                m_j = x if m_j is None else jnp.maximum(m_j, x)
            m_j = jnp.max(m_j, axis=0, keepdims=True)
            if j < i and b is not None:
                m_j = m_j + b
            m = m_j if m is None else jnp.maximum(m, m_j)
        for j, b in enumerate(biases):
            for r, rows in chunks(j):
                if j == i:
                    x = s_ref[buf, rows, :] + (causal_bias[r:r + MOBA_ROW_CHUNK] - m)
                else:
                    x = s_ref[buf, rows, :] - (m if b is None else m - b)
                p_ref[hh, rows, :] = jnp.exp2(x).astype(BF16)

    def weighted_values(i, hh):
        n_keys = (i + 1) * blk
        o_t = jnp.dot(vt_ref[:, 0:n_keys], p_ref[hh, 0:n_keys, :], preferred_element_type=F32)
        return o_t[0:LANES] * (1.0 / o_t[LANES:LANES + 1])

    pending = [scores(0, hh) for hh in range(2)]
    for i in range(n_blk):
        gates = pending
        if i + 1 < n_blk:
            pending = [scores(i + 1, hh) for hh in range(2)]
        for hh in range(2):
            softmax(i, hh, gates[hh])
        outs = [weighted_values(i, hh) for hh in range(2)]
        o_t = jnp.where(head_rows, outs[0], outs[1])
        o_ref[0, i * blk:(i + 1) * blk, :] = o_t.T.astype(o_ref.dtype)


def _log_one_minus_beta(zn):
    e = jnp.exp2(jnp.abs(zn) * (-LOG2E))
    return jnp.minimum(zn, 0.0) - jnp.log(1.0 + e)


def _suffix_sums(x, tri2):
    hi = x.astype(BF16)
    lo = (x - hi.astype(F32)).astype(BF16)
    return jnp.dot(jnp.concatenate([hi, lo], axis=1), tri2, preferred_element_type=F32)


def _stick_tiles(n_q, n_par):
    entries, first = [], [0]
    for d in range(n_q):
        group = [(i, i - d) for i in range(d, n_q)]
        entries += group + [(n_q, 0)] * (-len(group) % n_par)
        first.append(len(entries))
    return np.array(entries, np.int32).T, np.array(first, np.int32)


def _stick_kernel(tab_ref, first_ref, q_ref, k_ref, v_ref, o_ref, qn_ref, acc_ref, carry_ref,
                  *, tq, n_par):
    seq = q_ref.shape[1]
    n_q = seq // tq
    scale = HEAD_DIM ** -0.5
    nt = (((1,), (1,)), ((), ()))
    lane = lax.broadcasted_iota(jnp.int32, (1, LANES), 1)
    q_neg = q_ref[0].astype(F32) * (-scale)
    spare = pl.ds(seq, tq)
    for hh in range(2):
        qn_ref[hh, 0:seq, :] = jnp.where(lane // HEAD_DIM == hh, q_neg, 0.0).astype(BF16)
        qn_ref[hh, spare, :] = jnp.zeros((tq, LANES), BF16)
        acc_ref[hh, spare, :] = jnp.zeros((tq, LANES), F32)
        carry_ref[hh, spare, :] = jnp.zeros((tq, LANES), F32)
    row = lax.broadcasted_iota(jnp.int32, (tq, tq), 0)
    col = lax.broadcasted_iota(jnp.int32, (tq, tq), 1)
    strict = col < row
    tri = (row > col).astype(BF16)
    tri2 = jnp.concatenate([tri, tri], axis=0)

    def process(first, diag):
        chains = []
        for t in range(n_par):
            q_start = pl.multiple_of(tab_ref[0, first + t] * tq, tq)
            k_start = pl.multiple_of(tab_ref[1, first + t] * tq, tq)
            kc = k_ref[0, pl.ds(k_start, tq), :]
            vc = v_ref[0, pl.ds(k_start, tq), :]
            for hh in range(2):
                rows = (hh, pl.ds(q_start, tq), slice(None))
                zn = lax.dot_general(qn_ref[rows], kc, nt, preferred_element_type=F32)
                chains.append(dict(rows=rows, vc=vc, zn=zn))
        for ch in chains:
            log_1m = _log_one_minus_beta(ch["zn"])
            ch["log_1m"] = jnp.where(strict, log_1m, 0.0) if diag else log_1m
        for ch in chains:
            total = jnp.broadcast_to(jnp.sum(ch["log_1m"], axis=1, keepdims=True), (tq, LANES))
            if diag:
                carry_ref[ch["rows"]] = total
            else:
                ch["carry"] = carry_ref[ch["rows"]]
                carry_ref[ch["rows"]] = ch["carry"] + total
        for ch in chains:
            ch["after"] = _suffix_sums(ch["log_1m"], tri2)
        for ch in chains:
            log_a = (ch["log_1m"] - ch["zn"]) + ch["after"]
            if not diag:
                log_a = log_a + jnp.concatenate([ch["carry"]] * (tq // LANES), axis=1)
            att = jnp.exp2(log_a * LOG2E)
            if diag:
                att = jnp.where(strict, att, 0.0)
            ch["att"] = att.astype(BF16)
        for ch in chains:
            out = jnp.dot(ch["att"], ch["vc"], preferred_element_type=F32)
            if diag:
                acc_ref[ch["rows"]] = out
            else:
                acc_ref[ch["rows"]] += out

    def steps(d, diag):
        first = first_ref[d]

        def step(it, _):
            process(first + it * n_par, diag)
            return 0

        lax.fori_loop(0, (first_ref[d + 1] - first) // n_par, step, 0)

    def more_to_do(state):
        d, live = state
        return (d < n_q) & (live > STICK_DEAD_BELOW)

    def distance(state):
        d, _ = state
        steps(d, False)
        rows = lax.broadcasted_iota(jnp.int32, (seq, LANES), 0)
        carry = jnp.maximum(carry_ref[0, 0:seq, :], carry_ref[1, 0:seq, :])
        live = jnp.max(jnp.where(rows >= (d + 1) * tq, carry, -jnp.inf))
        return d + 1, live

    steps(0, True)
    lax.while_loop(more_to_do, distance, (jnp.int32(1), jnp.float32(0.0)))
    o_ref[0] = jnp.where(lane < HEAD_DIM, acc_ref[0, 0:seq, :],
                         acc_ref[1, 0:seq, :]).astype(o_ref.dtype)


def _attention(kind, q, k, v, bsz, seq):
    width = q.shape[1]
    q3, k3, v3 = (a.reshape(bsz, seq, width) for a in (q, k, v))
    spec = pl.BlockSpec((1, seq, LANES), lambda b, p: (b, 0, p))
    if kind == "moba":
        body = _moba_kernel
        scratch = [pltpu.VMEM((2, seq, LANES), F32),
                   pltpu.VMEM((2, seq, LANES), BF16),
                   pltpu.VMEM((seq, LANES), BF16),
                   pltpu.VMEM((LANES + 2 * SUBLANES, seq), BF16),
                   pltpu.VMEM((LANES, LANES), F32),
                   pltpu.VMEM((4, seq, MOBA_BLOCK), F32),
                   pltpu.VMEM((2, seq, MOBA_BLOCK), BF16)]
        in_specs, args = [spec, spec, spec], (q3, k3, v3)
    else:
        tq, n_par = STICK_TILE, STICK_TILES_PER_STEP
        tiles, first = _stick_tiles(seq // tq, n_par)
        body = functools.partial(_stick_kernel, tq=tq, n_par=n_par)
        assert q.dtype == k.dtype == v.dtype == BF16
        scratch = [pltpu.VMEM((2, seq + tq, LANES), BF16),
                   pltpu.VMEM((2, seq + tq, LANES), F32),
                   pltpu.VMEM((2, seq + tq, LANES), F32)]
        smem = pl.BlockSpec(memory_space=pltpu.SMEM)
        in_specs = [smem, smem, spec, spec, spec]
        args = (jnp.asarray(tiles), jnp.asarray(first), q3, k3, v3)
    out = pl.pallas_call(
        body,
        grid=(bsz, width // LANES),
        in_specs=in_specs,
        out_specs=spec,
        out_shape=jax.ShapeDtypeStruct((bsz, seq, width), BF16),
        scratch_shapes=scratch,
        compiler_params=_params(2),
        name=kind,
    )(*args)
    return out.reshape(bsz * seq, width)


def _conv_out_kernel(x_ref, oa_ref, ga_ref, gb_ref, hga_ref, hgb_ref, g_ref, cw_ref, cvec_ref,
                     w_ref, o_ref, ext_ref, win_ref):
    ts = x_ref.shape[0]
    first = pl.program_id(1) == 0
    halo = hga_ref[...].astype(F32) * _sigmoid(hgb_ref[...].astype(F32))
    ext_ref[0:CONV_HALO, :] = jnp.where(first, 0.0, halo)
    ext_ref[CONV_HALO:CONV_HALO + ts, :] = (ga_ref[...].astype(F32)
                                            * _sigmoid(gb_ref[...].astype(F32)))
    off = CONV_HALO - (CONV_K - 1)
    acc = jnp.zeros((ts, ga_ref.shape[1]), F32) + cvec_ref[0:1, :]
    for phase in range(SUBLANES):
        taps = [tap for tap in range(CONV_K) if (off + tap) % SUBLANES == phase]
        if not taps:
            continue
        span = (off + taps[-1]) // SUBLANES * SUBLANES + ts
        win_ref[0:span, :] = ext_ref[phase:phase + span, :]
        for tap in taps:
            start = (off + tap) // SUBLANES * SUBLANES
            acc = acc + win_ref[start:start + ts, :] * cw_ref[tap:tap + 1, :]
    y = _layernorm(acc, cvec_ref[1:2, :], cvec_ref[2:3, :])
    ob = y * _sigmoid(y)
    wa = oa_ref.shape[1]
    m = (jnp.dot(oa_ref[...], w_ref[0:wa, :], preferred_element_type=F32)
         + jnp.dot(ob.astype(BF16), w_ref[wa:, :], preferred_element_type=F32))
    o_ref[...] = x_ref[...] + _rms(m, g_ref[3:4, :])


def _conv_out(x2, oa, ga, gb, g, conv_w, conv_vecs, w_out, layer, bsz, seq, ts=1024):
    t, d = x2.shape
    par = (layer // 2,)
    wb = ga.shape[1]
    n_seq = seq // ts
    halo_per_tile = ts // CONV_HALO
    tile = lambda width: pl.BlockSpec((ts, width), lambda b, i: (b * n_seq + i, 0))
    halo = pl.BlockSpec(
        (CONV_HALO, wb),
        lambda b, i: (jnp.maximum((b * n_seq + i) * halo_per_tile - 1, 0), 0))
    return pl.pallas_call(
        _conv_out_kernel,
        grid=(bsz, n_seq),
        in_specs=[tile(d), tile(oa.shape[1]), tile(wb), tile(wb), halo, halo,
                  _resident(g, (layer,)), _resident(conv_w, par), _resident(conv_vecs, par),
                  _resident(w_out, par)],
        out_specs=tile(d),
        out_shape=jax.ShapeDtypeStruct((t, d), F32),
        scratch_shapes=[pltpu.VMEM((CONV_HALO + ts, wb), F32)] * 2,
        compiler_params=_params(2),
        name="conv_out",
    )(x2, oa, ga, gb, ga, gb, g, conv_w, conv_vecs, w_out)


def _sgu_out_kernel(x_ref, u_ref, vc_ref, od_ref, g_ref, lnv_ref, ws_ref, bs_ref, w_ref, o_ref,
                    oc_ref):
    ts = x_ref.shape[0]
    n_chunk = ts // SGU_CHUNK
    gw = vc_ref.shape[1] // C_GROUPS
    v = _layernorm(_gelu_tanh(vc_ref[...].astype(F32)), lnv_ref[0:1, :],
                   lnv_ref[1:2, :]).astype(BF16)
    row = lax.broadcasted_iota(jnp.int32, (SGU_CHUNK, SGU_CHUNK), 0)
    col = lax.broadcasted_iota(jnp.int32, (SGU_CHUNK, SGU_CHUNK), 1)
    for grp in range(C_GROUPS):
        w_s = jnp.where(col <= row, ws_ref[grp], 0.0).astype(BF16)
        lanes = slice(grp * gw, (grp + 1) * gw)
        vg = jnp.concatenate(
            [v[n * SGU_CHUNK:(n + 1) * SGU_CHUNK, lanes] for n in range(n_chunk)], axis=1)
        mixed = jnp.dot(w_s, vg, preferred_element_type=F32) + bs_ref[:, grp:grp + 1]
        for n in range(n_chunk):
            rows = slice(n * SGU_CHUNK, (n + 1) * SGU_CHUNK)
            oc_ref[rows, lanes] = (_gelu_tanh(u_ref[rows, lanes].astype(F32))
                                   * mixed[:, n * gw:(n + 1) * gw]).astype(BF16)
    wc = u_ref.shape[1]
    m = (jnp.dot(oc_ref[...], w_ref[0:wc, :], preferred_element_type=F32)
         + jnp.dot(od_ref[...], w_ref[wc:, :], preferred_element_type=F32))
    o_ref[...] = x_ref[...] + _rms(m, g_ref[3:4, :])


def _sgu_out(x2, u, vc, od, g, ln_vecs, w_s, b_s_t, w_out, layer, ts=1024):
    t, d = x2.shape
    wc = u.shape[1]
    par = (layer // 2,)
    tile = lambda width: pl.BlockSpec((ts, width), lambda i: (i, 0))
    return pl.pallas_call(
        _sgu_out_kernel,
        grid=(t // ts,),
        in_specs=[tile(d), tile(wc), tile(wc), tile(od.shape[1]),
                  _resident(g, (layer,)), _resident(ln_vecs, par), _resident(w_s, par),
                  _resident(b_s_t, par), _resident(w_out, par)],
        out_specs=tile(d),
        out_shape=jax.ShapeDtypeStruct((t, d), F32),
        scratch_shapes=[pltpu.VMEM((ts, wc), BF16)],
        compiler_params=_params(1),
        name="sgu_out",
    )(x2, u, vc, od, g, ln_vecs, w_s, b_s_t, w_out)


def _rope_tables(seq):
    half = HEAD_DIM // 2
    pos = jnp.arange(seq, dtype=F32)
    inv = ROPE_THETA ** (-jnp.arange(0, HEAD_DIM, 2, dtype=F32) / HEAD_DIM)
    ang = pos[:, None] * inv[None, :]
    cos, sin = jnp.cos(ang), jnp.sin(ang)
    reps = LANES // HEAD_DIM
    cos_t = jnp.tile(jnp.concatenate([cos, cos], axis=1), (1, reps))
    sin_t = jnp.tile(jnp.concatenate([-sin, sin], axis=1), (1, reps))
    return cos_t, sin_t


def kernel(x, norm_g, ffn_w_gate, ffn_w_up, ffn_w_down, ab_w_in, ab_w_out, conv_w, conv_b,
           conv_ln_g, conv_ln_b, cd_w_in, cd_w_out, sgu_ln_g, sgu_ln_b, sgu_w, sgu_b):
    bsz, seq, d = x.shape
    depth = norm_g.shape[0]
    rope_tabs = _rope_tables(seq)
    wg, wu, wd = (w.astype(BF16) for w in (ffn_w_gate, ffn_w_up, ffn_w_down))
    ab_in, ab_out, cd_in, cd_out = (w.astype(BF16) for w in (ab_w_in, ab_w_out, cd_w_in, cd_w_out))
    conv_vecs = jnp.stack([conv_b, conv_ln_g, conv_ln_b], axis=1)
    ln_vecs = jnp.stack([sgu_ln_g, sgu_ln_b], axis=1)
    sgu_b_t = jnp.swapaxes(sgu_b, 1, 2)
    x2 = x.reshape(bsz * seq, d)
    for layer in range(depth):
        x2 = _ffn(x2, norm_g, wg, wu, wd, layer, 0)
        if layer % 2 == 0:
            q, k, v, ga, gb = _inproj(x2, norm_g, ab_in, layer, seq, rope_tabs,
                                      [F32, F32, BF16, BF16, BF16])
            oa = _attention("moba", q, k, v, bsz, seq)
            x2 = _conv_out(x2, oa, ga, gb, norm_g, conv_w, conv_vecs, ab_out, layer, bsz, seq)
        else:
            u, vc, q, k, v = _inproj(x2, norm_g, cd_in, layer, seq, None, [BF16] * 5)
            od = _attention("stick", q, k, v, bsz, seq)
            x2 = _sgu_out(x2, u, vc, od, norm_g, ln_vecs, sgu_w, sgu_b_t, cd_out, layer)
        x2 = _ffn(x2, norm_g, wg, wu, wd, layer, 1)
    return x2.reshape(bsz, seq, d)
```

```python
import functools

import jax
import jax.numpy as jnp
import numpy as np
from jax import lax
from jax.experimental import pallas as pl
from jax.experimental.pallas import tpu as pltpu

HEAD_DIM = 64
MOBA_BLOCK = 256
MOBA_TOPK = 3
CONV_K = 31
SGU_CHUNK = 128
C_GROUPS = 4
ROPE_THETA = 10000.0
RMS_EPS = 1e-6
LN_EPS = 1e-5
LOG2E = 1.4426950408889634

LANES = 128
SUBLANES = 8
CONV_HALO = 32
MOBA_ROW_CHUNK = 32
STICK_TILE = 256
STICK_DEAD_BELOW = -110.0
VMEM_LIMIT = 56 * 1024 * 1024

F32 = jnp.float32
BF16 = jnp.bfloat16


def _rms(x, g):
    return x * lax.rsqrt(jnp.mean(x * x, axis=-1, keepdims=True) + RMS_EPS) * g


def _layernorm(x, g, b):
    mu = jnp.mean(x, axis=-1, keepdims=True)
    xc = x - mu
    var = jnp.mean(xc * xc, axis=-1, keepdims=True)
    return xc * lax.rsqrt(var + LN_EPS) * g + b


def _sigmoid(x):
    return 1.0 / (1.0 + jnp.exp(-x))


def _gelu_tanh(x):
    c = 0.7978845608028654
    return 0.5 * x * (1.0 + jnp.tanh(c * (x + 0.044715 * (x * x * x))))


def _resident(arr, lead):
    tail = arr.shape[len(lead):]
    index = tuple(lead) + (0,) * len(tail)
    return pl.BlockSpec((None,) * len(lead) + tail, lambda *_: index,
                        pipeline_mode=pl.Buffered(1))


def _params(n_grid):
    return pltpu.CompilerParams(
        dimension_semantics=("arbitrary",) * n_grid, vmem_limit_bytes=VMEM_LIMIT)


def _ffn_kernel(x_ref, g_ref, wg_ref, wu_ref, wd_ref, o_ref, act_ref, *, g_in, g_out, tf):
    n_sub, ts, d_ff = act_ref.shape

    def rows(s):
        return slice(s * ts, (s + 1) * ts)

    def normed_input(s):
        return _rms(x_ref[rows(s), :], g_ref[g_in:g_in + 1, :]).astype(BF16)

    def finish(s, f):
        o_ref[rows(s), :] = x_ref[rows(s), :] + 0.5 * _rms(f, g_ref[g_out:g_out + 1, :])

    h = normed_input(0)
    f_prev = None
    for s in range(n_sub):
        h_next = None
        for c in range(d_ff // tf):
            sl = slice(c * tf, (c + 1) * tf)
            gate = jnp.dot(h, wg_ref[:, sl].astype(BF16), preferred_element_type=F32)
            up = jnp.dot(h, wu_ref[:, sl].astype(BF16), preferred_element_type=F32)
            act_ref[s, :, sl] = (gate * _sigmoid(gate) * up).astype(BF16)
            if c == 0:
                if s + 1 < n_sub:
                    h_next = normed_input(s + 1)
                if f_prev is not None:
                    finish(s - 1, f_prev)
        f_prev = jnp.dot(act_ref[s], wd_ref[...], preferred_element_type=F32)
        h = h_next
    finish(n_sub - 1, f_prev)


def _ffn(x2, g, wg, wu, wd, layer, half, tm=1024, n_sub=2, tf=256):
    t, d = x2.shape
    d_ff = wg.shape[-1]
    g_in, g_out = (0, 1) if half == 0 else (4, 5)
    return pl.pallas_call(
        functools.partial(_ffn_kernel, g_in=g_in, g_out=g_out, tf=tf),
        grid=(t // tm,),
        in_specs=[
            pl.BlockSpec((tm, d), lambda i: (i, 0)),
            _resident(g, (layer,)),
            _resident(wg, (layer, half)),
            _resident(wu, (layer, half)),
            _resident(wd, (layer, half)),
        ],
        out_specs=pl.BlockSpec((tm, d), lambda i: (i, 0)),
        out_shape=jax.ShapeDtypeStruct((t, d), F32),
        scratch_shapes=[pltpu.VMEM((n_sub, tm // n_sub, d_ff), BF16)],
        compiler_params=_params(1),
        name="ffn",
    )(x2, g, wg, wu, wd)


def _rope(x, cos, sin_signed):
    half = HEAD_DIM // 2
    lane = lax.broadcasted_iota(jnp.int32, x.shape, 1)
    first_half = (lane % HEAD_DIM) < half
    partner = jnp.where(first_half, pltpu.roll(x, LANES - half, 1), pltpu.roll(x, half, 1))
    return x * cos + partner * sin_signed


def _inproj_kernel(x_ref, g_ref, w_ref, *rest, rope, width):
    if rope:
        cos_ref, sin_ref = rest[:2]
        outs = rest[2:]
    else:
        outs = rest
    h = _rms(x_ref[...], g_ref[2:3, :]).astype(BF16)
    for n, o_ref in enumerate(outs):
        y = jnp.dot(h, w_ref[:, n * width:(n + 1) * width].astype(BF16),
                    preferred_element_type=F32)
        if rope and n < 2:
            for t in range(width // LANES):
                sl = slice(t * LANES, (t + 1) * LANES)
                o_ref[:, sl] = _rope(y[:, sl], cos_ref[...], sin_ref[...]).astype(o_ref.dtype)
        else:
            o_ref[...] = y.astype(o_ref.dtype)


def _inproj(x2, g, w, layer, seq, rope_tabs, out_dtypes, tm=1024):
    t, d = x2.shape
    width = w.shape[-1] // len(out_dtypes)
    n_seq = seq // tm
    in_specs = [
        pl.BlockSpec((tm, d), lambda i: (i, 0)),
        _resident(g, (layer,)),
        _resident(w, (layer // 2,)),
    ]
    args = [x2, g, w]
    if rope_tabs is not None:
        in_specs += [pl.BlockSpec((tm, LANES), lambda i: (i % n_seq, 0))] * 2
        args += list(rope_tabs)
    return pl.pallas_call(
        functools.partial(_inproj_kernel, rope=rope_tabs is not None, width=width),
        grid=(t // tm,),
        in_specs=in_specs,
        out_specs=[pl.BlockSpec((tm, width), lambda i: (i, 0))] * len(out_dtypes),
        out_shape=[jax.ShapeDtypeStruct((t, width), dt) for dt in out_dtypes],
        compiler_params=_params(1),
        name="inproj_rope" if rope_tabs is not None else "inproj",
    )(*args)


def _moba_select_bias(gate_t, own):
    sub = lax.broadcasted_iota(jnp.int32, gate_t.shape, 0)
    past = sub < own
    gm = jnp.where(past, gate_t, -jnp.inf)
    rank = jnp.zeros(gate_t.shape, jnp.int32)
    for i in range(own):
        gi = gm[i:i + 1, :]
        beats = (gi > gm) | ((gi == gm) & (sub > i))
        rank = rank + jnp.where(beats, 1, 0)
    return jnp.where(past & (rank < MOBA_TOPK), 0.0, -jnp.inf)


def _moba_kernel(q_ref, k_ref, v_ref, o_ref, qm_ref, qs_ref, kb_ref, vt_ref, km_ref, s_ref,
                 p_ref):
    seq = q_ref.shape[1]
    blk = MOBA_BLOCK
    n_blk = seq // blk
    n_sub = 8 * pl.cdiv(n_blk, 8)
    scale = HEAD_DIM ** -0.5
    nt = (((1,), (1,)), ((), ()))
    kb_ref[...] = k_ref[0].astype(BF16)
    vt_ref[0:LANES, :] = v_ref[0].astype(F32).T.astype(BF16)
    vt_ref[LANES:, :] = jnp.ones((2 * SUBLANES, seq), BF16)
    km_ref[...] = jnp.zeros(km_ref.shape, F32)
    for j in range(n_blk):
        km_ref[j:j + 1, :] = jnp.mean(k_ref[0, j * blk:(j + 1) * blk, :], axis=0, keepdims=True)
    lane = lax.broadcasted_iota(jnp.int32, (1, LANES), 1)
    for hh in range(2):
        qm = jnp.where(lane // HEAD_DIM == hh, q_ref[0], 0.0)
        qm_ref[hh] = qm
        qs_ref[hh] = (qm * (scale * LOG2E)).astype(BF16)
    key = lax.broadcasted_iota(jnp.int32, (blk, blk), 0)
    qry = lax.broadcasted_iota(jnp.int32, (blk, blk), 1)
    causal_bias = jnp.where(key <= qry, 0.0, -jnp.inf)
    head_rows = lax.broadcasted_iota(jnp.int32, (LANES, 1), 0) < HEAD_DIM

    def scores(i, hh):
        rows = slice(i * blk, (i + 1) * blk)
        n_keys = (i + 1) * blk
        s_ref[2 * (i % 2) + hh, 0:n_keys, :] = lax.dot_general(
            kb_ref[0:n_keys, :], qs_ref[hh, rows, :], nt, preferred_element_type=F32)
        if i <= MOBA_TOPK:
            return None
        return lax.dot_general(km_ref[...], qm_ref[hh, rows, :], nt,
                               precision=lax.Precision.HIGHEST,
                               preferred_element_type=F32)[0:n_sub]

    def softmax(i, hh, gate_t):
        buf = 2 * (i % 2) + hh
        biases = [None] * i + [causal_bias]
        if gate_t is not None:
            bias = _moba_select_bias(gate_t, i)
            biases = [bias[j:j + 1, :] for j in range(i)] + [causal_bias]

        def chunks(j):
            for r in range(0, blk, MOBA_ROW_CHUNK):
                yield r, slice(j * blk + r, j * blk + r + MOBA_ROW_CHUNK)

        m = None
        for j, b in enumerate(biases):
            m_j = None
            for r, rows in chunks(j):
                x = s_ref[buf, rows, :]
                if j == i:
                    x = x + causal_bias[r:r + MOBA_ROW_CHUNK]
                m_j = x if m_j is None else jnp.maximum(m_j, x)
            m_j = jnp.max(m_j, axis=0, keepdims=True)
            if j < i and b is not None:
                m_j = m_j + b
            m = m_j if m is None else jnp.maximum(m, m_j)
        for j, b in enumerate(biases):
            for r, rows in chunks(j):
                if j == i:
                    x = s_ref[buf, rows, :] + (causal_bias[r:r + MOBA_ROW_CHUNK] - m)
                else:
                    x = s_ref[buf, rows, :] - (m if b is None else m - b)
                p_ref[hh, rows, :] = jnp.exp2(x).astype(BF16)

    def weighted_values(i, hh):
        n_keys = (i + 1) * blk
        o_t = jnp.dot(vt_ref[:, 0:n_keys], p_ref[hh, 0:n_keys, :], preferred_element_type=F32)
        return o_t[0:LANES] * (1.0 / o_t[LANES:LANES + 1])

    pending = [scores(0, hh) for hh in range(2)]
    for i in range(n_blk):
        gates = pending
        if i + 1 < n_blk:
            pending = [scores(i + 1, hh) for hh in range(2)]
        for hh in range(2):
            softmax(i, hh, gates[hh])
        outs = [weighted_values(i, hh) for hh in range(2)]
        o_t = jnp.where(head_rows, outs[0], outs[1])
        o_ref[0, i * blk:(i + 1) * blk, :] = o_t.T.astype(o_ref.dtype)


def _log_one_minus_beta(zn):
    e = jnp.exp2(jnp.abs(zn) * (-LOG2E))
    return jnp.minimum(zn, 0.0) - jnp.log(1.0 + e)


def _suffix_sums(x, tri2):
    hi = x.astype(BF16)
    lo = (x - hi.astype(F32)).astype(BF16)
    return jnp.dot(jnp.concatenate([hi, lo], axis=1), tri2, preferred_element_type=F32)


def _stick_tiles(n_q):
    entries, first = [], [0]
    for d in range(n_q):
        n_par = n_q if d == 0 else n_q - 1
        group = [(i, i - d) for i in range(d, n_q)]
        entries += group + [(n_q, 0)] * (-len(group) % n_par)
        first.append(len(entries))
    return np.array(entries, np.int32).T, np.array(first, np.int32)


def _stick_kernel(tab_ref, first_ref, q_ref, k_ref, v_ref, o_ref, qn_ref, acc_ref, carry_ref,
                  *, tq):
    seq = q_ref.shape[1]
    n_q = seq // tq
    scale = HEAD_DIM ** -0.5
    nt = (((1,), (1,)), ((), ()))
    lane = lax.broadcasted_iota(jnp.int32, (1, LANES), 1)
    q_neg = q_ref[0].astype(F32) * (-scale)
    spare = pl.ds(seq, tq)
    for hh in range(2):
        qn_ref[hh, 0:seq, :] = jnp.where(lane // HEAD_DIM == hh, q_neg, 0.0).astype(BF16)
        qn_ref[hh, spare, :] = jnp.zeros((tq, LANES), BF16)
        acc_ref[hh, spare, :] = jnp.zeros((tq, LANES), F32)
        carry_ref[hh, spare, :] = jnp.zeros((tq, LANES), F32)
    row = lax.broadcasted_iota(jnp.int32, (tq, tq), 0)
    col = lax.broadcasted_iota(jnp.int32, (tq, tq), 1)
    strict = col < row
    tri = (row > col).astype(BF16)
    tri2 = jnp.concatenate([tri, tri], axis=0)

    def tiles_per_step(diag):
        return n_q if diag else n_q - 1

    def process(first, diag):
        chains = []
        for t in range(tiles_per_step(diag)):
            q_start = pl.multiple_of(tab_ref[0, first + t] * tq, tq)
            k_start = pl.multiple_of(tab_ref[1, first + t] * tq, tq)
            kc = k_ref[0, pl.ds(k_start, tq), :]
            vc = v_ref[0, pl.ds(k_start, tq), :]
            for hh in range(2):
                rows = (hh, pl.ds(q_start, tq), slice(None))
                zn = lax.dot_general(qn_ref[rows], kc, nt, preferred_element_type=F32)
                chains.append(dict(rows=rows, vc=vc, zn=zn))
        for ch in chains:
            log_1m = _log_one_minus_beta(ch["zn"])
            ch["log_1m"] = jnp.where(strict, log_1m, 0.0) if diag else log_1m
        for ch in chains:
            total = jnp.broadcast_to(jnp.sum(ch["log_1m"], axis=1, keepdims=True), (tq, LANES))
            if diag:
                carry_ref[ch["rows"]] = total
            else:
                ch["carry"] = carry_ref[ch["rows"]]
                carry_ref[ch["rows"]] = ch["carry"] + total
        for ch in chains:
            ch["after"] = _suffix_sums(ch["log_1m"], tri2)
        for ch in chains:
            log_a = (ch["log_1m"] - ch["zn"]) + ch["after"]
            if not diag:
                log_a = log_a + jnp.concatenate([ch["carry"]] * (tq // LANES), axis=1)
            att = jnp.exp2(log_a * LOG2E)
            if diag:
                att = jnp.where(strict, att, 0.0)
            ch["att"] = att.astype(BF16)
        for ch in chains:
            out = jnp.dot(ch["att"], ch["vc"], preferred_element_type=F32)
            if diag:
                acc_ref[ch["rows"]] = out
            else:
                acc_ref[ch["rows"]] += out

    def steps(d, diag):
        first = first_ref[d]
        n_par = tiles_per_step(diag)

        def step(it, _):
            process(first + it * n_par, diag)
            return 0

        lax.fori_loop(0, (first_ref[d + 1] - first) // n_par, step, 0)

    def more_to_do(state):
        d, live = state
        return (d < n_q) & (live > STICK_DEAD_BELOW)

    def distance(state):
        d, _ = state
        steps(d, False)
        rows = lax.broadcasted_iota(jnp.int32, (seq, LANES), 0)
        carry = jnp.maximum(carry_ref[0, 0:seq, :], carry_ref[1, 0:seq, :])
        live = jnp.max(jnp.where(rows >= (d + 1) * tq, carry, -jnp.inf))
        return d + 1, live

    steps(0, True)
    lax.while_loop(more_to_do, distance, (jnp.int32(1), jnp.float32(0.0)))
    o_ref[0] = jnp.where(lane < HEAD_DIM, acc_ref[0, 0:seq, :],
                         acc_ref[1, 0:seq, :]).astype(o_ref.dtype)


def _attention(kind, q, k, v, bsz, seq):
    width = q.shape[1]
    q3, k3, v3 = (a.reshape(bsz, seq, width) for a in (q, k, v))
    spec = pl.BlockSpec((1, seq, LANES), lambda b, p: (b, 0, p))
    if kind == "moba":
        body = _moba_kernel
        scratch = [pltpu.VMEM((2, seq, LANES), F32),
                   pltpu.VMEM((2, seq, LANES), BF16),
                   pltpu.VMEM((seq, LANES), BF16),
                   pltpu.VMEM((LANES + 2 * SUBLANES, seq), BF16),
                   pltpu.VMEM((LANES, LANES), F32),
                   pltpu.VMEM((4, seq, MOBA_BLOCK), F32),
                   pltpu.VMEM((2, seq, MOBA_BLOCK), BF16)]
        in_specs, args = [spec, spec, spec], (q3, k3, v3)
    else:
        tq = STICK_TILE
        tiles, first = _stick_tiles(seq // tq)
        body = functools.partial(_stick_kernel, tq=tq)
        assert q.dtype == k.dtype == v.dtype == BF16
        scratch = [pltpu.VMEM((2, seq + tq, LANES), BF16),
                   pltpu.VMEM((2, seq + tq, LANES), F32),
                   pltpu.VMEM((2, seq + tq, LANES), F32)]
        smem = pl.BlockSpec(memory_space=pltpu.SMEM)
        in_specs = [smem, smem, spec, spec, spec]
        args = (jnp.asarray(tiles), jnp.asarray(first), q3, k3, v3)
    out = pl.pallas_call(
        body,
        grid=(bsz, width // LANES),
        in_specs=in_specs,
        out_specs=spec,
        out_shape=jax.ShapeDtypeStruct((bsz, seq, width), BF16),
        scratch_shapes=scratch,
        compiler_params=_params(2),
        name=kind,
    )(*args)
    return out.reshape(bsz * seq, width)


def _conv_out_kernel(x_ref, oa_ref, ga_ref, gb_ref, hga_ref, hgb_ref, g_ref, cw_ref, cvec_ref,
                     w_ref, o_ref, ext_ref, win_ref):
    ts = x_ref.shape[0]
    first = pl.program_id(1) == 0
    halo = hga_ref[...].astype(F32) * _sigmoid(hgb_ref[...].astype(F32))
    ext_ref[0:CONV_HALO, :] = jnp.where(first, 0.0, halo)
    ext_ref[CONV_HALO:CONV_HALO + ts, :] = (ga_ref[...].astype(F32)
                                            * _sigmoid(gb_ref[...].astype(F32)))
    off = CONV_HALO - (CONV_K - 1)
    acc = jnp.zeros((ts, ga_ref.shape[1]), F32) + cvec_ref[0:1, :]
    for phase in range(SUBLANES):
        taps = [tap for tap in range(CONV_K) if (off + tap) % SUBLANES == phase]
        if not taps:
            continue
        span = (off + taps[-1]) // SUBLANES * SUBLANES + ts
        win_ref[0:span, :] = ext_ref[phase:phase + span, :]
        for tap in taps:
            start = (off + tap) // SUBLANES * SUBLANES
            acc = acc + win_ref[start:start + ts, :] * cw_ref[tap:tap + 1, :]
    y = _layernorm(acc, cvec_ref[1:2, :], cvec_ref[2:3, :])
    ob = y * _sigmoid(y)
    wa = oa_ref.shape[1]
    m = (jnp.dot(oa_ref[...], w_ref[0:wa, :], preferred_element_type=F32)
         + jnp.dot(ob.astype(BF16), w_ref[wa:, :], preferred_element_type=F32))
    o_ref[...] = x_ref[...] + _rms(m, g_ref[3:4, :])


def _conv_out(x2, oa, ga, gb, g, conv_w, conv_vecs, w_out, layer, bsz, seq, ts=1024):
    t, d = x2.shape
    par = (layer // 2,)
    wb = ga.shape[1]
    n_seq = seq // ts
    halo_per_tile = ts // CONV_HALO
    tile = lambda width: pl.BlockSpec((ts, width), lambda b, i: (b * n_seq + i, 0))
    halo = pl.BlockSpec(
        (CONV_HALO, wb),
        lambda b, i: (jnp.maximum((b * n_seq + i) * halo_per_tile - 1, 0), 0))
    return pl.pallas_call(
        _conv_out_kernel,
        grid=(bsz, n_seq),
        in_specs=[tile(d), tile(oa.shape[1]), tile(wb), tile(wb), halo, halo,
                  _resident(g, (layer,)), _resident(conv_w, par), _resident(conv_vecs, par),
                  _resident(w_out, par)],
        out_specs=tile(d),
        out_shape=jax.ShapeDtypeStruct((t, d), F32),
        scratch_shapes=[pltpu.VMEM((CONV_HALO + ts, wb), F32)] * 2,
        compiler_params=_params(2),
        name="conv_out",
    )(x2, oa, ga, gb, ga, gb, g, conv_w, conv_vecs, w_out)


def _sgu_out_kernel(x_ref, u_ref, vc_ref, od_ref, g_ref, lnv_ref, ws_ref, bs_ref, w_ref, o_ref,
                    oc_ref):
    ts = x_ref.shape[0]
    n_chunk = ts // SGU_CHUNK
    gw = vc_ref.shape[1] // C_GROUPS
    v = _layernorm(_gelu_tanh(vc_ref[...].astype(F32)), lnv_ref[0:1, :],
                   lnv_ref[1:2, :]).astype(BF16)
    row = lax.broadcasted_iota(jnp.int32, (SGU_CHUNK, SGU_CHUNK), 0)
    col = lax.broadcasted_iota(jnp.int32, (SGU_CHUNK, SGU_CHUNK), 1)
    for grp in range(C_GROUPS):
        w_s = jnp.where(col <= row, ws_ref[grp], 0.0).astype(BF16)
        lanes = slice(grp * gw, (grp + 1) * gw)
        vg = jnp.concatenate(
            [v[n * SGU_CHUNK:(n + 1) * SGU_CHUNK, lanes] for n in range(n_chunk)], axis=1)
        mixed = jnp.dot(w_s, vg, preferred_element_type=F32) + bs_ref[:, grp:grp + 1]
        for n in range(n_chunk):
            rows = slice(n * SGU_CHUNK, (n + 1) * SGU_CHUNK)
            oc_ref[rows, lanes] = (_gelu_tanh(u_ref[rows, lanes].astype(F32))
                                   * mixed[:, n * gw:(n + 1) * gw]).astype(BF16)
    wc = u_ref.shape[1]
    m = (jnp.dot(oc_ref[...], w_ref[0:wc, :], preferred_element_type=F32)
         + jnp.dot(od_ref[...], w_ref[wc:, :], preferred_element_type=F32))
    o_ref[...] = x_ref[...] + _rms(m, g_ref[3:4, :])


def _sgu_out(x2, u, vc, od, g, ln_vecs, w_s, b_s_t, w_out, layer, ts=1024):
    t, d = x2.shape
    wc = u.shape[1]
    par = (layer // 2,)
    tile = lambda width: pl.BlockSpec((ts, width), lambda i: (i, 0))
    return pl.pallas_call(
        _sgu_out_kernel,
        grid=(t // ts,),
        in_specs=[tile(d), tile(wc), tile(wc), tile(od.shape[1]),
                  _resident(g, (layer,)), _resident(ln_vecs, par), _resident(w_s, par),
                  _resident(b_s_t, par), _resident(w_out, par)],
        out_specs=tile(d),
        out_shape=jax.ShapeDtypeStruct((t, d), F32),
        scratch_shapes=[pltpu.VMEM((ts, wc), BF16)],
        compiler_params=_params(1),
        name="sgu_out",
    )(x2, u, vc, od, g, ln_vecs, w_s, b_s_t, w_out)


def _rope_tables(seq):
    half = HEAD_DIM // 2
    pos = jnp.arange(seq, dtype=F32)
    inv = ROPE_THETA ** (-jnp.arange(0, HEAD_DIM, 2, dtype=F32) / HEAD_DIM)
    ang = pos[:, None] * inv[None, :]
    cos, sin = jnp.cos(ang), jnp.sin(ang)
    reps = LANES // HEAD_DIM
    cos_t = jnp.tile(jnp.concatenate([cos, cos], axis=1), (1, reps))
    sin_t = jnp.tile(jnp.concatenate([-sin, sin], axis=1), (1, reps))
    return cos_t, sin_t


def kernel(x, norm_g, ffn_w_gate, ffn_w_up, ffn_w_down, ab_w_in, ab_w_out, conv_w, conv_b,
           conv_ln_g, conv_ln_b, cd_w_in, cd_w_out, sgu_ln_g, sgu_ln_b, sgu_w, sgu_b):
    bsz, seq, d = x.shape
    depth = norm_g.shape[0]
    rope_tabs = _rope_tables(seq)
    wg, wu, wd = ffn_w_gate, ffn_w_up, ffn_w_down.astype(BF16)
    ab_in, cd_in = ab_w_in, cd_w_in
    ab_out, cd_out = ab_w_out.astype(BF16), cd_w_out.astype(BF16)
    conv_vecs = jnp.stack([conv_b, conv_ln_g, conv_ln_b], axis=1)
    ln_vecs = jnp.stack([sgu_ln_g, sgu_ln_b], axis=1)
    sgu_b_t = jnp.swapaxes(sgu_b, 1, 2)
    x2 = x.reshape(bsz * seq, d)
    for layer in range(depth):
        x2 = _ffn(x2, norm_g, wg, wu, wd, layer, 0)
        if layer % 2 == 0:
            q, k, v, ga, gb = _inproj(x2, norm_g, ab_in, layer, seq, rope_tabs,
                                      [F32, F32, BF16, BF16, BF16])
            oa = _attention("moba", q, k, v, bsz, seq)
            x2 = _conv_out(x2, oa, ga, gb, norm_g, conv_w, conv_vecs, ab_out, layer, bsz, seq)
        else:
            u, vc, q, k, v = _inproj(x2, norm_g, cd_in, layer, seq, None, [BF16] * 5)
            od = _attention("stick", q, k, v, bsz, seq)
            x2 = _sgu_out(x2, u, vc, od, norm_g, ln_vecs, sgu_w, sgu_b_t, cd_out, layer)
        x2 = _ffn(x2, norm_g, wg, wu, wd, layer, 1)
    return x2.reshape(bsz, seq, d)
```

```python
import functools

import jax
import jax.numpy as jnp
import numpy as np
from jax import lax
from jax.experimental import pallas as pl
from jax.experimental.pallas import tpu as pltpu

HEAD_DIM = 64
MOBA_BLOCK = 256
MOBA_TOPK = 3
CONV_K = 31
SGU_CHUNK = 128
C_GROUPS = 4
ROPE_THETA = 10000.0
RMS_EPS = 1e-6
LN_EPS = 1e-5
LOG2E = 1.4426950408889634

LANES = 128
SUBLANES = 8
CONV_HALO = 32
MOBA_ROW_CHUNK = 32
STICK_TILE = 256
STICK_DEAD_BELOW = -110.0
VMEM_LIMIT = 60 * 1024 * 1024

F32 = jnp.float32
BF16 = jnp.bfloat16


def _rms(x, g):
    return x * lax.rsqrt(jnp.mean(x * x, axis=-1, keepdims=True) + RMS_EPS) * g


def _layernorm(x, g, b):
    mu = jnp.mean(x, axis=-1, keepdims=True)
    xc = x - mu
    var = jnp.mean(xc * xc, axis=-1, keepdims=True)
    return xc * lax.rsqrt(var + LN_EPS) * g + b


def _sigmoid(x):
    return 1.0 / (1.0 + jnp.exp(-x))


def _gelu_tanh(x):
    c = 0.7978845608028654
    return 0.5 * x * (1.0 + jnp.tanh(c * (x + 0.044715 * (x * x * x))))


def _resident(arr, lead):
    tail = arr.shape[len(lead):]
    index = tuple(lead) + (0,) * len(tail)
    return pl.BlockSpec((None,) * len(lead) + tail, lambda *_: index,
                        pipeline_mode=pl.Buffered(1))


def _params(n_grid):
    return pltpu.CompilerParams(
        dimension_semantics=("arbitrary",) * n_grid, vmem_limit_bytes=VMEM_LIMIT)


def _ffn_kernel(x_ref, g_ref, wg_ref, wu_ref, wd_ref, o_ref, act_ref, *, g_in, g_out, tf):
    n_sub, ts, d_ff = act_ref.shape

    def rows(s):
        return slice(s * ts, (s + 1) * ts)

    def normed_input(s):
        return _rms(x_ref[rows(s), :], g_ref[g_in:g_in + 1, :]).astype(BF16)

    def finish(s, f):
        o_ref[rows(s), :] = x_ref[rows(s), :] + 0.5 * _rms(f, g_ref[g_out:g_out + 1, :])

    h = normed_input(0)
    f_prev = None
    for s in range(n_sub):
        h_next = None
        for c in range(d_ff // tf):
            sl = slice(c * tf, (c + 1) * tf)
            gate = jnp.dot(h, wg_ref[:, sl].astype(BF16), preferred_element_type=F32)
            up = jnp.dot(h, wu_ref[:, sl].astype(BF16), preferred_element_type=F32)
            act_ref[s, :, sl] = (gate * _sigmoid(gate) * up).astype(BF16)
            if c == 0:
                if s + 1 < n_sub:
                    h_next = normed_input(s + 1)
                if f_prev is not None:
                    finish(s - 1, f_prev)
        f_prev = jnp.dot(act_ref[s], wd_ref[...].astype(BF16), preferred_element_type=F32)
        h = h_next
    finish(n_sub - 1, f_prev)


def _ffn(x2, g, wg, wu, wd, layer, half, tm=1024, n_sub=2, tf=256):
    t, d = x2.shape
    d_ff = wg.shape[-1]
    g_in, g_out = (0, 1) if half == 0 else (4, 5)
    return pl.pallas_call(
        functools.partial(_ffn_kernel, g_in=g_in, g_out=g_out, tf=tf),
        grid=(t // tm,),
        in_specs=[
            pl.BlockSpec((tm, d), lambda i: (i, 0)),
            _resident(g, (layer,)),
            _resident(wg, (layer, half)),
            _resident(wu, (layer, half)),
            _resident(wd, (layer, half)),
        ],
        out_specs=pl.BlockSpec((tm, d), lambda i: (i, 0)),
        out_shape=jax.ShapeDtypeStruct((t, d), F32),
        scratch_shapes=[pltpu.VMEM((n_sub, tm // n_sub, d_ff), BF16)],
        compiler_params=_params(1),
        name="ffn",
    )(x2, g, wg, wu, wd)


def _rope(x, cos, sin_signed):
    half = HEAD_DIM // 2
    lane = lax.broadcasted_iota(jnp.int32, x.shape, 1)
    first_half = (lane % HEAD_DIM) < half
    partner = jnp.where(first_half, pltpu.roll(x, LANES - half, 1), pltpu.roll(x, half, 1))
    return x * cos + partner * sin_signed


def _inproj_kernel(x_ref, g_ref, w_ref, *rest, rope, width):
    if rope:
        cos_ref, sin_ref = rest[:2]
        outs = rest[2:]
    else:
        outs = rest
    h = _rms(x_ref[...], g_ref[2:3, :]).astype(BF16)
    for n, o_ref in enumerate(outs):
        y = jnp.dot(h, w_ref[:, n * width:(n + 1) * width].astype(BF16),
                    preferred_element_type=F32)
        if rope and n < 2:
            for t in range(width // LANES):
                sl = slice(t * LANES, (t + 1) * LANES)
                o_ref[:, sl] = _rope(y[:, sl], cos_ref[...], sin_ref[...]).astype(o_ref.dtype)
        else:
            o_ref[...] = y.astype(o_ref.dtype)


def _inproj(x2, g, w, layer, seq, rope_tabs, out_dtypes, tm=1024):
    t, d = x2.shape
    width = w.shape[-1] // len(out_dtypes)
    n_seq = seq // tm
    in_specs = [
        pl.BlockSpec((tm, d), lambda i: (i, 0)),
        _resident(g, (layer,)),
        _resident(w, (layer // 2,)),
    ]
    args = [x2, g, w]
    if rope_tabs is not None:
        in_specs += [pl.BlockSpec((tm, LANES), lambda i: (i % n_seq, 0))] * 2
        args += list(rope_tabs)
    return pl.pallas_call(
        functools.partial(_inproj_kernel, rope=rope_tabs is not None, width=width),
        grid=(t // tm,),
        in_specs=in_specs,
        out_specs=[pl.BlockSpec((tm, width), lambda i: (i, 0))] * len(out_dtypes),
        out_shape=[jax.ShapeDtypeStruct((t, width), dt) for dt in out_dtypes],
        compiler_params=_params(1),
        name="inproj_rope" if rope_tabs is not None else "inproj",
    )(*args)


def _moba_select_bias(gate_t, own):
    sub = lax.broadcasted_iota(jnp.int32, gate_t.shape, 0)
    past = sub < own
    gm = jnp.where(past, gate_t, -jnp.inf)
    rank = jnp.zeros(gate_t.shape, jnp.int32)
    for i in range(own):
        gi = gm[i:i + 1, :]
        beats = (gi > gm) | ((gi == gm) & (sub > i))
        rank = rank + jnp.where(beats, 1, 0)
    return jnp.where(past & (rank < MOBA_TOPK), 0.0, -jnp.inf)


def _moba_kernel(q_ref, k_ref, v_ref, o_ref, qm_ref, qs_ref, kb_ref, vt_ref, km_ref, s_ref,
                 p_ref):
    seq = q_ref.shape[1]
    blk = MOBA_BLOCK
    n_blk = seq // blk
    n_sub = 8 * pl.cdiv(n_blk, 8)
    scale = HEAD_DIM ** -0.5
    nt = (((1,), (1,)), ((), ()))
    kb_ref[...] = k_ref[0].astype(BF16)
    vt_ref[0:LANES, :] = v_ref[0].astype(F32).T.astype(BF16)
    vt_ref[LANES:, :] = jnp.ones((2 * SUBLANES, seq), BF16)
    km_ref[...] = jnp.zeros(km_ref.shape, F32)
    for j in range(n_blk):
        km_ref[j:j + 1, :] = jnp.mean(k_ref[0, j * blk:(j + 1) * blk, :], axis=0, keepdims=True)
    lane = lax.broadcasted_iota(jnp.int32, (1, LANES), 1)
    for hh in range(2):
        qm = jnp.where(lane // HEAD_DIM == hh, q_ref[0], 0.0)
        qm_ref[hh] = qm
        qs_ref[hh] = (qm * (scale * LOG2E)).astype(BF16)
    key = lax.broadcasted_iota(jnp.int32, (blk, blk), 0)
    qry = lax.broadcasted_iota(jnp.int32, (blk, blk), 1)
    causal_bias = jnp.where(key <= qry, 0.0, -jnp.inf)
    head_rows = lax.broadcasted_iota(jnp.int32, (LANES, 1), 0) < HEAD_DIM

    def scores(i, hh):
        rows = slice(i * blk, (i + 1) * blk)
        n_keys = (i + 1) * blk
        s_ref[2 * (i % 2) + hh, 0:n_keys, :] = lax.dot_general(
            kb_ref[0:n_keys, :], qs_ref[hh, rows, :], nt, preferred_element_type=F32)
        if i <= MOBA_TOPK:
            return None
        return lax.dot_general(km_ref[...], qm_ref[hh, rows, :], nt,
                               precision=lax.Precision.HIGHEST,
                               preferred_element_type=F32)[0:n_sub]

    def softmax(i, hh, gate_t):
        buf = 2 * (i % 2) + hh
        biases = [None] * i + [causal_bias]
        if gate_t is not None:
            bias = _moba_select_bias(gate_t, i)
            biases = [bias[j:j + 1, :] for j in range(i)] + [causal_bias]

        def chunks(j):
            for r in range(0, blk, MOBA_ROW_CHUNK):
                yield r, slice(j * blk + r, j * blk + r + MOBA_ROW_CHUNK)

        m = None
        for j, b in enumerate(biases):
            m_j = None
            for r, rows in chunks(j):
                x = s_ref[buf, rows, :]
                if j == i:
                    x = x + causal_bias[r:r + MOBA_ROW_CHUNK]
                m_j = x if m_j is None else jnp.maximum(m_j, x)
            m_j = jnp.max(m_j, axis=0, keepdims=True)
            if j < i and b is not None:
                m_j = m_j + b
            m = m_j if m is None else jnp.maximum(m, m_j)
        for j, b in enumerate(biases):
            for r, rows in chunks(j):
                if j == i:
                    x = s_ref[buf, rows, :] + (causal_bias[r:r + MOBA_ROW_CHUNK] - m)
                else:
                    x = s_ref[buf, rows, :] - (m if b is None else m - b)
                p_ref[hh, rows, :] = jnp.exp2(x).astype(BF16)

    def weighted_values(i, hh):
        n_keys = (i + 1) * blk
        o_t = jnp.dot(vt_ref[:, 0:n_keys], p_ref[hh, 0:n_keys, :], preferred_element_type=F32)
        return o_t[0:LANES] * (1.0 / o_t[LANES:LANES + 1])

    pending = [scores(0, hh) for hh in range(2)]
    for i in range(n_blk):
        gates = pending
        if i + 1 < n_blk:
            pending = [scores(i + 1, hh) for hh in range(2)]
        for hh in range(2):
            softmax(i, hh, gates[hh])
        outs = [weighted_values(i, hh) for hh in range(2)]
        o_t = jnp.where(head_rows, outs[0], outs[1])
        o_ref[0, i * blk:(i + 1) * blk, :] = o_t.T.astype(o_ref.dtype)


def _log_one_minus_beta(zn):
    e = jnp.exp2(jnp.abs(zn) * (-LOG2E))
    return jnp.minimum(zn, 0.0) - jnp.log(1.0 + e)


def _suffix_sums(x, tri2):
    hi = x.astype(BF16)
    lo = (x - hi.astype(F32)).astype(BF16)
    return jnp.dot(jnp.concatenate([hi, lo], axis=1), tri2, preferred_element_type=F32)


def _stick_tiles(n_q):
    entries, first = [], [0]
    for d in range(n_q):
        n_par = n_q if d == 0 else n_q - 1
        group = [(i, i - d) for i in range(d, n_q)]
        entries += group + [(n_q, 0)] * (-len(group) % n_par)
        first.append(len(entries))
    return np.array(entries, np.int32).T, np.array(first, np.int32)


def _stick_kernel(tab_ref, first_ref, q_ref, k_ref, v_ref, o_ref, qn_ref, acc_ref, carry_ref,
                  live_ref, *, tq):
    seq = q_ref.shape[1]
    n_q = seq // tq
    scale = HEAD_DIM ** -0.5
    nt = (((1,), (1,)), ((), ()))
    lane = lax.broadcasted_iota(jnp.int32, (1, LANES), 1)
    q_neg = q_ref[0].astype(F32) * (-scale)
    spare = pl.ds(seq, tq)
    for hh in range(2):
        qn_ref[hh, 0:seq, :] = jnp.where(lane // HEAD_DIM == hh, q_neg, 0.0).astype(BF16)
        qn_ref[hh, spare, :] = jnp.zeros((tq, LANES), BF16)
        acc_ref[hh, spare, :] = jnp.zeros((tq, LANES), F32)
        carry_ref[hh, spare, :] = jnp.zeros((tq, LANES), F32)
    row = lax.broadcasted_iota(jnp.int32, (tq, tq), 0)
    col = lax.broadcasted_iota(jnp.int32, (tq, tq), 1)
    strict = col < row
    tri = (row > col).astype(BF16)
    tri2 = jnp.concatenate([tri, tri], axis=0)

    def tiles_per_step(diag):
        return n_q if diag else n_q - 1

    def process(first, diag):
        chains = []
        for t in range(tiles_per_step(diag)):
            q_start = pl.multiple_of(tab_ref[0, first + t] * tq, tq)
            k_start = pl.multiple_of(tab_ref[1, first + t] * tq, tq)
            kc = k_ref[0, pl.ds(k_start, tq), :]
            vc = v_ref[0, pl.ds(k_start, tq), :]
            for hh in range(2):
                rows = (hh, pl.ds(q_start, tq), slice(None))
                zn = lax.dot_general(qn_ref[rows], kc, nt, preferred_element_type=F32)
                chains.append(dict(rows=rows, vc=vc, zn=zn, more_left=k_start > 0))
        for ch in chains:
            log_1m = _log_one_minus_beta(ch["zn"])
            ch["log_1m"] = jnp.where(strict, log_1m, 0.0) if diag else log_1m
        for ch in chains:
            total = jnp.broadcast_to(jnp.sum(ch["log_1m"], axis=1, keepdims=True), (tq, LANES))
            if diag:
                carry_ref[ch["rows"]] = total
            else:
                ch["carry"] = carry_ref[ch["rows"]]
                new_carry = ch["carry"] + total
                carry_ref[ch["rows"]] = new_carry
                top = jnp.max(new_carry, axis=0, keepdims=True)
                live_ref[...] = jnp.maximum(live_ref[...],
                                            jnp.where(ch["more_left"], top, -jnp.inf))
        for ch in chains:
            ch["after"] = _suffix_sums(ch["log_1m"], tri2)
        for ch in chains:
            log_a = (ch["log_1m"] - ch["zn"]) + ch["after"]
            if not diag:
                log_a = log_a + jnp.concatenate([ch["carry"]] * (tq // LANES), axis=1)
            att = jnp.exp2(log_a * LOG2E)
            if diag:
                att = jnp.where(strict, att, 0.0)
            ch["att"] = att.astype(BF16)
        for ch in chains:
            out = jnp.dot(ch["att"], ch["vc"], preferred_element_type=F32)
            if diag:
                acc_ref[ch["rows"]] = out
            else:
                acc_ref[ch["rows"]] += out

    def steps(d, diag):
        first = first_ref[d]
        n_par = tiles_per_step(diag)

        def step(it, _):
            process(first + it * n_par, diag)
            return 0

        lax.fori_loop(0, (first_ref[d + 1] - first) // n_par, step, 0)

    def more_to_do(state):
        d, live = state
        return (d < n_q) & (live > STICK_DEAD_BELOW)

    def distance(state):
        d, _ = state
        live_ref[...] = jnp.full(live_ref.shape, -jnp.inf, F32)
        steps(d, False)
        return d + 1, jnp.max(live_ref[...])

    steps(0, True)
    lax.while_loop(more_to_do, distance, (jnp.int32(1), jnp.float32(0.0)))
    o_ref[0] = jnp.where(lane < HEAD_DIM, acc_ref[0, 0:seq, :],
                         acc_ref[1, 0:seq, :]).astype(o_ref.dtype)


def _attention(kind, q, k, v, bsz, seq):
    width = q.shape[1]
    q3, k3, v3 = (a.reshape(bsz, seq, width) for a in (q, k, v))
    spec = pl.BlockSpec((1, seq, LANES), lambda b, p: (b, 0, p))
    if kind == "moba":
        body = _moba_kernel
        scratch = [pltpu.VMEM((2, seq, LANES), F32),
                   pltpu.VMEM((2, seq, LANES), BF16),
                   pltpu.VMEM((seq, LANES), BF16),
                   pltpu.VMEM((LANES + 2 * SUBLANES, seq), BF16),
                   pltpu.VMEM((LANES, LANES), F32),
                   pltpu.VMEM((4, seq, MOBA_BLOCK), F32),
                   pltpu.VMEM((2, seq, MOBA_BLOCK), BF16)]
        in_specs, args = [spec, spec, spec], (q3, k3, v3)
    else:
        tq = STICK_TILE
        tiles, first = _stick_tiles(seq // tq)
        body = functools.partial(_stick_kernel, tq=tq)
        assert q.dtype == k.dtype == v.dtype == BF16
        scratch = [pltpu.VMEM((2, seq + tq, LANES), BF16),
                   pltpu.VMEM((2, seq + tq, LANES), F32),
                   pltpu.VMEM((2, seq + tq, LANES), F32),
                   pltpu.VMEM((1, LANES), F32)]
        smem = pl.BlockSpec(memory_space=pltpu.SMEM)
        in_specs = [smem, smem, spec, spec, spec]
        args = (jnp.asarray(tiles), jnp.asarray(first), q3, k3, v3)
    out = pl.pallas_call(
        body,
        grid=(bsz, width // LANES),
        in_specs=in_specs,
        out_specs=spec,
        out_shape=jax.ShapeDtypeStruct((bsz, seq, width), BF16),
        scratch_shapes=scratch,
        compiler_params=_params(2),
        name=kind,
    )(*args)
    return out.reshape(bsz * seq, width)


def _conv_out_kernel(x_ref, oa_ref, ga_ref, gb_ref, hga_ref, hgb_ref, g_ref, cw_ref, cvec_ref,
                     w_ref, o_ref, ext_ref, win_ref):
    ts = x_ref.shape[0]
    first = pl.program_id(1) == 0
    halo = hga_ref[...].astype(F32) * _sigmoid(hgb_ref[...].astype(F32))
    ext_ref[0:CONV_HALO, :] = jnp.where(first, 0.0, halo)
    ext_ref[CONV_HALO:CONV_HALO + ts, :] = (ga_ref[...].astype(F32)
                                            * _sigmoid(gb_ref[...].astype(F32)))
    off = CONV_HALO - (CONV_K - 1)
    acc = jnp.zeros((ts, ga_ref.shape[1]), F32) + cvec_ref[0:1, :]
    for phase in range(SUBLANES):
        taps = [tap for tap in range(CONV_K) if (off + tap) % SUBLANES == phase]
        if not taps:
            continue
        span = (off + taps[-1]) // SUBLANES * SUBLANES + ts
        win_ref[0:span, :] = ext_ref[phase:phase + span, :]
        for tap in taps:
            start = (off + tap) // SUBLANES * SUBLANES
            acc = acc + win_ref[start:start + ts, :] * cw_ref[tap:tap + 1, :]
    y = _layernorm(acc, cvec_ref[1:2, :], cvec_ref[2:3, :])
    ob = y * _sigmoid(y)
    wa = oa_ref.shape[1]
    m = (jnp.dot(oa_ref[...], w_ref[0:wa, :], preferred_element_type=F32)
         + jnp.dot(ob.astype(BF16), w_ref[wa:, :], preferred_element_type=F32))
    o_ref[...] = x_ref[...] + _rms(m, g_ref[3:4, :])


def _conv_out(x2, oa, ga, gb, g, conv_w, conv_vecs, w_out, layer, bsz, seq, ts=1024):
    t, d = x2.shape
    par = (layer // 2,)
    wb = ga.shape[1]
    n_seq = seq // ts
    halo_per_tile = ts // CONV_HALO
    tile = lambda width: pl.BlockSpec((ts, width), lambda b, i: (b * n_seq + i, 0))
    halo = pl.BlockSpec(
        (CONV_HALO, wb),
        lambda b, i: (jnp.maximum((b * n_seq + i) * halo_per_tile - 1, 0), 0))
    return pl.pallas_call(
        _conv_out_kernel,
        grid=(bsz, n_seq),
        in_specs=[tile(d), tile(oa.shape[1]), tile(wb), tile(wb), halo, halo,
                  _resident(g, (layer,)), _resident(conv_w, par), _resident(conv_vecs, par),
                  _resident(w_out, par)],
        out_specs=tile(d),
        out_shape=jax.ShapeDtypeStruct((t, d), F32),
        scratch_shapes=[pltpu.VMEM((CONV_HALO + ts, wb), F32)] * 2,
        compiler_params=_params(2),
        name="conv_out",
    )(x2, oa, ga, gb, ga, gb, g, conv_w, conv_vecs, w_out)


def _sgu_out_kernel(x_ref, u_ref, vc_ref, od_ref, g_ref, lnv_ref, ws_ref, bs_ref, w_ref, o_ref,
                    oc_ref):
    ts = x_ref.shape[0]
    n_chunk = ts // SGU_CHUNK
    gw = vc_ref.shape[1] // C_GROUPS
    v = _layernorm(_gelu_tanh(vc_ref[...].astype(F32)), lnv_ref[0:1, :],
                   lnv_ref[1:2, :]).astype(BF16)
    row = lax.broadcasted_iota(jnp.int32, (SGU_CHUNK, SGU_CHUNK), 0)
    col = lax.broadcasted_iota(jnp.int32, (SGU_CHUNK, SGU_CHUNK), 1)
    for grp in range(C_GROUPS):
        w_s = jnp.where(col <= row, ws_ref[grp], 0.0).astype(BF16)
        lanes = slice(grp * gw, (grp + 1) * gw)
        vg = jnp.concatenate(
            [v[n * SGU_CHUNK:(n + 1) * SGU_CHUNK, lanes] for n in range(n_chunk)], axis=1)
        mixed = jnp.dot(w_s, vg, preferred_element_type=F32) + bs_ref[:, grp:grp + 1]
        for n in range(n_chunk):
            rows = slice(n * SGU_CHUNK, (n + 1) * SGU_CHUNK)
            oc_ref[rows, lanes] = (_gelu_tanh(u_ref[rows, lanes].astype(F32))
                                   * mixed[:, n * gw:(n + 1) * gw]).astype(BF16)
    wc = u_ref.shape[1]
    m = (jnp.dot(oc_ref[...], w_ref[0:wc, :], preferred_element_type=F32)
         + jnp.dot(od_ref[...], w_ref[wc:, :], preferred_element_type=F32))
    o_ref[...] = x_ref[...] + _rms(m, g_ref[3:4, :])


def _sgu_out(x2, u, vc, od, g, ln_vecs, w_s, b_s_t, w_out, layer, ts=1024):
    t, d = x2.shape
    wc = u.shape[1]
    par = (layer // 2,)
    tile = lambda width: pl.BlockSpec((ts, width), lambda i: (i, 0))
    return pl.pallas_call(
        _sgu_out_kernel,
        grid=(t // ts,),
        in_specs=[tile(d), tile(wc), tile(wc), tile(od.shape[1]),
                  _resident(g, (layer,)), _resident(ln_vecs, par), _resident(w_s, par),
                  _resident(b_s_t, par), _resident(w_out, par)],
        out_specs=tile(d),
        out_shape=jax.ShapeDtypeStruct((t, d), F32),
        scratch_shapes=[pltpu.VMEM((ts, wc), BF16)],
        compiler_params=_params(1),
        name="sgu_out",
    )(x2, u, vc, od, g, ln_vecs, w_s, b_s_t, w_out)


def _rope_tables(seq):
    half = HEAD_DIM // 2
    pos = jnp.arange(seq, dtype=F32)
    inv = ROPE_THETA ** (-jnp.arange(0, HEAD_DIM, 2, dtype=F32) / HEAD_DIM)
    ang = pos[:, None] * inv[None, :]
    cos, sin = jnp.cos(ang), jnp.sin(ang)
    reps = LANES // HEAD_DIM
    cos_t = jnp.tile(jnp.concatenate([cos, cos], axis=1), (1, reps))
    sin_t = jnp.tile(jnp.concatenate([-sin, sin], axis=1), (1, reps))
    return cos_t, sin_t


def kernel(x, norm_g, ffn_w_gate, ffn_w_up, ffn_w_down, ab_w_in, ab_w_out, conv_w, conv_b,
           conv_ln_g, conv_ln_b, cd_w_in, cd_w_out, sgu_ln_g, sgu_ln_b, sgu_w, sgu_b):
    bsz, seq, d = x.shape
    depth = norm_g.shape[0]
    rope_tabs = _rope_tables(seq)
    wg, wu, wd = ffn_w_gate, ffn_w_up, ffn_w_down
    ab_in, cd_in = ab_w_in, cd_w_in
    ab_out, cd_out = ab_w_out.astype(BF16), cd_w_out.astype(BF16)
    conv_vecs = jnp.stack([conv_b, conv_ln_g, conv_ln_b], axis=1)
    ln_vecs = jnp.stack([sgu_ln_g, sgu_ln_b], axis=1)
    sgu_b_t = jnp.swapaxes(sgu_b, 1, 2)
    x2 = x.reshape(bsz * seq, d)
    for layer in range(depth):
        x2 = _ffn(x2, norm_g, wg, wu, wd, layer, 0)
        if layer % 2 == 0:
            q, k, v, ga, gb = _inproj(x2, norm_g, ab_in, layer, seq, rope_tabs,
                                      [F32, F32, BF16, BF16, BF16])
            oa = _attention("moba", q, k, v, bsz, seq)
            x2 = _conv_out(x2, oa, ga, gb, norm_g, conv_w, conv_vecs, ab_out, layer, bsz, seq)
        else:
            u, vc, q, k, v = _inproj(x2, norm_g, cd_in, layer, seq, None, [BF16] * 5)
            od = _attention("stick", q, k, v, bsz, seq)
            x2 = _sgu_out(x2, u, vc, od, norm_g, ln_vecs, sgu_w, sgu_b_t, cd_out, layer)
        x2 = _ffn(x2, norm_g, wg, wu, wd, layer, 1)
    return x2.reshape(bsz, seq, d)
```

```python
import functools

import jax
import jax.numpy as jnp
import numpy as np
from jax import lax
from jax.experimental import pallas as pl
from jax.experimental.pallas import tpu as pltpu

HEAD_DIM = 64
MOBA_BLOCK = 256
MOBA_TOPK = 3
CONV_K = 31
SGU_CHUNK = 128
C_GROUPS = 4
ROPE_THETA = 10000.0
RMS_EPS = 1e-6
LN_EPS = 1e-5
LOG2E = 1.4426950408889634

LANES = 128
SUBLANES = 8
CONV_HALO = 32
MOBA_ROW_CHUNK = 32
STICK_TILE = 256
STICK_DEAD_BELOW = -110.0
VMEM_LIMIT = 60 * 1024 * 1024

F32 = jnp.float32
BF16 = jnp.bfloat16


def _rms(x, g):
    return x * lax.rsqrt(jnp.mean(x * x, axis=-1, keepdims=True) + RMS_EPS) * g


def _layernorm(x, g, b):
    mu = jnp.mean(x, axis=-1, keepdims=True)
    xc = x - mu
    var = jnp.mean(xc * xc, axis=-1, keepdims=True)
    return xc * lax.rsqrt(var + LN_EPS) * g + b


def _sigmoid(x):
    return 1.0 / (1.0 + jnp.exp(-x))


def _gelu_tanh(x):
    c = 0.7978845608028654
    return 0.5 * x * (1.0 + jnp.tanh(c * (x + 0.044715 * (x * x * x))))


def _resident(arr, lead):
    tail = arr.shape[len(lead):]
    index = tuple(lead) + (0,) * len(tail)
    return pl.BlockSpec((None,) * len(lead) + tail, lambda *_: index,
                        pipeline_mode=pl.Buffered(1))


def _params(n_grid):
    return pltpu.CompilerParams(
        dimension_semantics=("arbitrary",) * n_grid, vmem_limit_bytes=VMEM_LIMIT)


def _ffn_kernel(x_ref, g_ref, wg_ref, wu_ref, wd_ref, o_ref, act_ref, *, g_in, g_out, tf):
    n_sub, ts, d_ff = act_ref.shape

    def rows(s):
        return slice(s * ts, (s + 1) * ts)

    def normed_input(s):
        return _rms(x_ref[rows(s), :], g_ref[g_in:g_in + 1, :]).astype(BF16)

    def finish(s, f):
        o_ref[rows(s), :] = x_ref[rows(s), :] + 0.5 * _rms(f, g_ref[g_out:g_out + 1, :])

    h = normed_input(0)
    f_prev = None
    for s in range(n_sub):
        h_next = None
        for c in range(d_ff // tf):
            sl = slice(c * tf, (c + 1) * tf)
            gate = jnp.dot(h, wg_ref[:, sl].astype(BF16), preferred_element_type=F32)
            up = jnp.dot(h, wu_ref[:, sl].astype(BF16), preferred_element_type=F32)
            act_ref[s, :, sl] = (gate * _sigmoid(gate) * up).astype(BF16)
            if c == 0:
                if s + 1 < n_sub:
                    h_next = normed_input(s + 1)
                if f_prev is not None:
                    finish(s - 1, f_prev)
        f_prev = jnp.dot(act_ref[s], wd_ref[...].astype(BF16), preferred_element_type=F32)
        h = h_next
    finish(n_sub - 1, f_prev)


def _ffn(x2, g, wg, wu, wd, layer, half, tm=1024, n_sub=2, tf=256):
    t, d = x2.shape
    d_ff = wg.shape[-1]
    g_in, g_out = (0, 1) if half == 0 else (4, 5)
    return pl.pallas_call(
        functools.partial(_ffn_kernel, g_in=g_in, g_out=g_out, tf=tf),
        grid=(t // tm,),
        in_specs=[
            pl.BlockSpec((tm, d), lambda i: (i, 0)),
            _resident(g, (layer,)),
            _resident(wg, (layer, half)),
            _resident(wu, (layer, half)),
            _resident(wd, (layer, half)),
        ],
        out_specs=pl.BlockSpec((tm, d), lambda i: (i, 0)),
        out_shape=jax.ShapeDtypeStruct((t, d), F32),
        scratch_shapes=[pltpu.VMEM((n_sub, tm // n_sub, d_ff), BF16)],
        compiler_params=_params(1),
        name="ffn",
    )(x2, g, wg, wu, wd)


def _rope(x, cos, sin_signed):
    half = HEAD_DIM // 2
    lane = lax.broadcasted_iota(jnp.int32, x.shape, 1)
    first_half = (lane % HEAD_DIM) < half
    partner = jnp.where(first_half, pltpu.roll(x, LANES - half, 1), pltpu.roll(x, half, 1))
    return x * cos + partner * sin_signed


def _inproj_kernel(x_ref, g_ref, w_ref, *rest, rope, width):
    if rope:
        cos_ref, sin_ref = rest[:2]
        outs = rest[2:]
    else:
        outs = rest
    h = _rms(x_ref[...], g_ref[2:3, :]).astype(BF16)
    proj = lambda n: jnp.dot(h, w_ref[:, n * width:(n + 1) * width].astype(BF16),
                             preferred_element_type=F32)
    for n, o_ref in enumerate(outs):
        if rope and n == 3:
            o_ref[...] = (proj(3) * _sigmoid(proj(4))).astype(o_ref.dtype)
            break
        y = proj(n)
        if rope and n < 2:
            for t in range(width // LANES):
                sl = slice(t * LANES, (t + 1) * LANES)
                o_ref[:, sl] = _rope(y[:, sl], cos_ref[...], sin_ref[...]).astype(o_ref.dtype)
        else:
            o_ref[...] = y.astype(o_ref.dtype)


def _inproj(x2, g, w, layer, seq, rope_tabs, out_dtypes, tm=1024):
    t, d = x2.shape
    width = w.shape[-1] // 5
    n_seq = seq // tm
    in_specs = [
        pl.BlockSpec((tm, d), lambda i: (i, 0)),
        _resident(g, (layer,)),
        _resident(w, (layer // 2,)),
    ]
    args = [x2, g, w]
    if rope_tabs is not None:
        in_specs += [pl.BlockSpec((tm, LANES), lambda i: (i % n_seq, 0))] * 2
        args += list(rope_tabs)
    return pl.pallas_call(
        functools.partial(_inproj_kernel, rope=rope_tabs is not None, width=width),
        grid=(t // tm,),
        in_specs=in_specs,
        out_specs=[pl.BlockSpec((tm, width), lambda i: (i, 0))] * len(out_dtypes),
        out_shape=[jax.ShapeDtypeStruct((t, width), dt) for dt in out_dtypes],
        compiler_params=_params(1),
        name="inproj_rope" if rope_tabs is not None else "inproj",
    )(*args)


def _moba_select_bias(gate_t, own):
    sub = lax.broadcasted_iota(jnp.int32, gate_t.shape, 0)
    past = sub < own
    gm = jnp.where(past, gate_t, -jnp.inf)
    rank = jnp.zeros(gate_t.shape, jnp.int32)
    for i in range(own):
        gi = gm[i:i + 1, :]
        beats = (gi > gm) | ((gi == gm) & (sub > i))
        rank = rank + jnp.where(beats, 1, 0)
    return jnp.where(past & (rank < MOBA_TOPK), 0.0, -jnp.inf)


def _moba_kernel(q_ref, k_ref, v_ref, o_ref, qm_ref, qs_ref, kb_ref, vt_ref, km_ref, s_ref,
                 p_ref):
    seq = q_ref.shape[1]
    blk = MOBA_BLOCK
    n_blk = seq // blk
    n_sub = 8 * pl.cdiv(n_blk, 8)
    scale = HEAD_DIM ** -0.5
    nt = (((1,), (1,)), ((), ()))
    kb_ref[...] = k_ref[0].astype(BF16)
    vt_ref[0:LANES, :] = v_ref[0].astype(F32).T.astype(BF16)
    vt_ref[LANES:, :] = jnp.ones((2 * SUBLANES, seq), BF16)
    km_ref[...] = jnp.zeros(km_ref.shape, F32)
    for j in range(n_blk):
        km_ref[j:j + 1, :] = jnp.mean(k_ref[0, j * blk:(j + 1) * blk, :], axis=0, keepdims=True)
    lane = lax.broadcasted_iota(jnp.int32, (1, LANES), 1)
    for hh in range(2):
        qm = jnp.where(lane // HEAD_DIM == hh, q_ref[0], 0.0)
        qm_ref[hh] = qm
        qs_ref[hh] = (qm * (scale * LOG2E)).astype(BF16)
    key = lax.broadcasted_iota(jnp.int32, (blk, blk), 0)
    qry = lax.broadcasted_iota(jnp.int32, (blk, blk), 1)
    causal_bias = jnp.where(key <= qry, 0.0, -jnp.inf)
    head_rows = lax.broadcasted_iota(jnp.int32, (LANES, 1), 0) < HEAD_DIM

    def scores(i, hh):
        rows = slice(i * blk, (i + 1) * blk)
        n_keys = (i + 1) * blk
        s_ref[2 * (i % 2) + hh, 0:n_keys, :] = lax.dot_general(
            kb_ref[0:n_keys, :], qs_ref[hh, rows, :], nt, preferred_element_type=F32)
        if i <= MOBA_TOPK:
            return None
        return lax.dot_general(km_ref[...], qm_ref[hh, rows, :], nt,
                               precision=lax.Precision.HIGHEST,
                               preferred_element_type=F32)[0:n_sub]

    def softmax(i, hh, gate_t):
        buf = 2 * (i % 2) + hh
        biases = [None] * i + [causal_bias]
        if gate_t is not None:
            bias = _moba_select_bias(gate_t, i)
            biases = [bias[j:j + 1, :] for j in range(i)] + [causal_bias]

        def chunks(j):
            for r in range(0, blk, MOBA_ROW_CHUNK):
                yield r, slice(j * blk + r, j * blk + r + MOBA_ROW_CHUNK)

        m = None
        for j, b in enumerate(biases):
            m_j = None
            for r, rows in chunks(j):
                x = s_ref[buf, rows, :]
                if j == i:
                    x = x + causal_bias[r:r + MOBA_ROW_CHUNK]
                m_j = x if m_j is None else jnp.maximum(m_j, x)
            m_j = jnp.max(m_j, axis=0, keepdims=True)
            if j < i and b is not None:
                m_j = m_j + b
            m = m_j if m is None else jnp.maximum(m, m_j)
        for j, b in enumerate(biases):
            for r, rows in chunks(j):
                if j == i:
                    x = s_ref[buf, rows, :] + (causal_bias[r:r + MOBA_ROW_CHUNK] - m)
                else:
                    x = s_ref[buf, rows, :] - (m if b is None else m - b)
                p_ref[hh, rows, :] = jnp.exp2(x).astype(BF16)

    def weighted_values(i, hh):
        n_keys = (i + 1) * blk
        o_t = jnp.dot(vt_ref[:, 0:n_keys], p_ref[hh, 0:n_keys, :], preferred_element_type=F32)
        return o_t[0:LANES] * (1.0 / o_t[LANES:LANES + 1])

    pending = [scores(0, hh) for hh in range(2)]
    for i in range(n_blk):
        gates = pending
        if i + 1 < n_blk:
            pending = [scores(i + 1, hh) for hh in range(2)]
        for hh in range(2):
            softmax(i, hh, gates[hh])
        outs = [weighted_values(i, hh) for hh in range(2)]
        o_t = jnp.where(head_rows, outs[0], outs[1])
        o_ref[0, i * blk:(i + 1) * blk, :] = o_t.T.astype(o_ref.dtype)


def _log_one_minus_beta(zn):
    e = jnp.exp2(jnp.abs(zn) * (-LOG2E))
    return jnp.minimum(zn, 0.0) - jnp.log(1.0 + e)


def _suffix_sums(x, tri2):
    hi = x.astype(BF16)
    lo = (x - hi.astype(F32)).astype(BF16)
    return jnp.dot(jnp.concatenate([hi, lo], axis=1), tri2, preferred_element_type=F32)


def _stick_tiles(n_q):
    entries, first = [], [0]
    for d in range(n_q):
        n_par = n_q if d == 0 else n_q - 1
        group = [(i, i - d) for i in range(d, n_q)]
        entries += group + [(n_q, 0)] * (-len(group) % n_par)
        first.append(len(entries))
    return np.array(entries, np.int32).T, np.array(first, np.int32)


def _stick_kernel(tab_ref, first_ref, q_ref, k_ref, v_ref, o_ref, qn_ref, acc_ref, carry_ref,
                  live_ref, *, tq):
    seq = q_ref.shape[1]
    n_q = seq // tq
    scale = HEAD_DIM ** -0.5
    nt = (((1,), (1,)), ((), ()))
    lane = lax.broadcasted_iota(jnp.int32, (1, LANES), 1)
    q_neg = q_ref[0].astype(F32) * (-scale)
    spare = pl.ds(seq, tq)
    for hh in range(2):
        qn_ref[hh, 0:seq, :] = jnp.where(lane // HEAD_DIM == hh, q_neg, 0.0).astype(BF16)
        qn_ref[hh, spare, :] = jnp.zeros((tq, LANES), BF16)
        acc_ref[hh, spare, :] = jnp.zeros((tq, LANES), F32)
        carry_ref[hh, spare, :] = jnp.zeros((tq, LANES), F32)
    row = lax.broadcasted_iota(jnp.int32, (tq, tq), 0)
    col = lax.broadcasted_iota(jnp.int32, (tq, tq), 1)
    strict = col < row
    tri = (row > col).astype(BF16)
    tri2 = jnp.concatenate([tri, tri], axis=0)

    def tiles_per_step(diag):
        return n_q if diag else n_q - 1

    def process(first, diag):
        chains = []
        for t in range(tiles_per_step(diag)):
            q_start = pl.multiple_of(tab_ref[0, first + t] * tq, tq)
            k_start = pl.multiple_of(tab_ref[1, first + t] * tq, tq)
            kc = k_ref[0, pl.ds(k_start, tq), :]
            vc = v_ref[0, pl.ds(k_start, tq), :]
            for hh in range(2):
                rows = (hh, pl.ds(q_start, tq), slice(None))
                zn = lax.dot_general(qn_ref[rows], kc, nt, preferred_element_type=F32)
                chains.append(dict(rows=rows, vc=vc, zn=zn, more_left=k_start > 0))
        for ch in chains:
            log_1m = _log_one_minus_beta(ch["zn"])
            ch["log_1m"] = jnp.where(strict, log_1m, 0.0) if diag else log_1m
        for ch in chains:
            total = jnp.broadcast_to(jnp.sum(ch["log_1m"], axis=1, keepdims=True), (tq, LANES))
            if diag:
                carry_ref[ch["rows"]] = total
            else:
                ch["carry"] = carry_ref[ch["rows"]]
                new_carry = ch["carry"] + total
                carry_ref[ch["rows"]] = new_carry
                top = jnp.max(new_carry, axis=0, keepdims=True)
                live_ref[...] = jnp.maximum(live_ref[...],
                                            jnp.where(ch["more_left"], top, -jnp.inf))
        for ch in chains:
            ch["after"] = _suffix_sums(ch["log_1m"], tri2)
        for ch in chains:
            log_a = (ch["log_1m"] - ch["zn"]) + ch["after"]
            if not diag:
                log_a = log_a + jnp.concatenate([ch["carry"]] * (tq // LANES), axis=1)
            att = jnp.exp2(log_a * LOG2E)
            if diag:
                att = jnp.where(strict, att, 0.0)
            ch["att"] = att.astype(BF16)
        for ch in chains:
            out = jnp.dot(ch["att"], ch["vc"], preferred_element_type=F32)
            if diag:
                acc_ref[ch["rows"]] = out
            else:
                acc_ref[ch["rows"]] += out

    def steps(d, diag):
        first = first_ref[d]
        n_par = tiles_per_step(diag)

        def step(it, _):
            process(first + it * n_par, diag)
            return 0

        lax.fori_loop(0, (first_ref[d + 1] - first) // n_par, step, 0)

    def more_to_do(state):
        d, live = state
        return (d < n_q) & (live > STICK_DEAD_BELOW)

    def distance(state):
        d, _ = state
        live_ref[...] = jnp.full(live_ref.shape, -jnp.inf, F32)
        steps(d, False)
        return d + 1, jnp.max(live_ref[...])

    steps(0, True)
    lax.while_loop(more_to_do, distance, (jnp.int32(1), jnp.float32(0.0)))
    o_ref[0] = jnp.where(lane < HEAD_DIM, acc_ref[0, 0:seq, :],
                         acc_ref[1, 0:seq, :]).astype(o_ref.dtype)


def _attention(kind, q, k, v, bsz, seq):
    width = q.shape[1]
    q3, k3, v3 = (a.reshape(bsz, seq, width) for a in (q, k, v))
    spec = pl.BlockSpec((1, seq, LANES), lambda b, p: (b, 0, p))
    if kind == "moba":
        body = _moba_kernel
        scratch = [pltpu.VMEM((2, seq, LANES), F32),
                   pltpu.VMEM((2, seq, LANES), BF16),
                   pltpu.VMEM((seq, LANES), BF16),
                   pltpu.VMEM((LANES + 2 * SUBLANES, seq), BF16),
                   pltpu.VMEM((LANES, LANES), F32),
                   pltpu.VMEM((4, seq, MOBA_BLOCK), F32),
                   pltpu.VMEM((2, seq, MOBA_BLOCK), BF16)]
        in_specs, args = [spec, spec, spec], (q3, k3, v3)
    else:
        tq = STICK_TILE
        tiles, first = _stick_tiles(seq // tq)
        body = functools.partial(_stick_kernel, tq=tq)
        assert q.dtype == k.dtype == v.dtype == BF16
        scratch = [pltpu.VMEM((2, seq + tq, LANES), BF16),
                   pltpu.VMEM((2, seq + tq, LANES), F32),
                   pltpu.VMEM((2, seq + tq, LANES), F32),
                   pltpu.VMEM((1, LANES), F32)]
        smem = pl.BlockSpec(memory_space=pltpu.SMEM)
        in_specs = [smem, smem, spec, spec, spec]
        args = (jnp.asarray(tiles), jnp.asarray(first), q3, k3, v3)
    out = pl.pallas_call(
        body,
        grid=(bsz, width // LANES),
        in_specs=in_specs,
        out_specs=spec,
        out_shape=jax.ShapeDtypeStruct((bsz, seq, width), BF16),
        scratch_shapes=scratch,
        compiler_params=_params(2),
        name=kind,
    )(*args)
    return out.reshape(bsz * seq, width)


def _conv_out_kernel(x_ref, oa_ref, hc_ref, halo_ref, g_ref, cw_ref, cvec_ref, w_ref, o_ref,
                     ext_ref, win_ref):
    ts = x_ref.shape[0]
    first = pl.program_id(1) == 0
    ext_ref[0:CONV_HALO, :] = jnp.where(first, 0.0, halo_ref[...].astype(F32))
    ext_ref[CONV_HALO:CONV_HALO + ts, :] = hc_ref[...].astype(F32)
    off = CONV_HALO - (CONV_K - 1)
    acc = jnp.zeros((ts, hc_ref.shape[1]), F32) + cvec_ref[0:1, :]
    for phase in range(SUBLANES):
        taps = [tap for tap in range(CONV_K) if (off + tap) % SUBLANES == phase]
        if not taps:
            continue
        span = (off + taps[-1]) // SUBLANES * SUBLANES + ts
        win_ref[0:span, :] = ext_ref[phase:phase + span, :]
        for tap in taps:
            start = (off + tap) // SUBLANES * SUBLANES
            acc = acc + win_ref[start:start + ts, :] * cw_ref[tap:tap + 1, :]
    y = _layernorm(acc, cvec_ref[1:2, :], cvec_ref[2:3, :])
    ob = y * _sigmoid(y)
    wa = oa_ref.shape[1]
    m = (jnp.dot(oa_ref[...], w_ref[0:wa, :], preferred_element_type=F32)
         + jnp.dot(ob.astype(BF16), w_ref[wa:, :], preferred_element_type=F32))
    o_ref[...] = x_ref[...] + _rms(m, g_ref[3:4, :])


def _conv_out(x2, oa, hc, g, conv_w, conv_vecs, w_out, layer, bsz, seq, ts=1024):
    t, d = x2.shape
    par = (layer // 2,)
    wb = hc.shape[1]
    n_seq = seq // ts
    halo_per_tile = ts // CONV_HALO
    tile = lambda width: pl.BlockSpec((ts, width), lambda b, i: (b * n_seq + i, 0))
    halo = pl.BlockSpec(
        (CONV_HALO, wb),
        lambda b, i: (jnp.maximum((b * n_seq + i) * halo_per_tile - 1, 0), 0))
    return pl.pallas_call(
        _conv_out_kernel,
        grid=(bsz, n_seq),
        in_specs=[tile(d), tile(oa.shape[1]), tile(wb), halo,
                  _resident(g, (layer,)), _resident(conv_w, par), _resident(conv_vecs, par),
                  _resident(w_out, par)],
        out_specs=tile(d),
        out_shape=jax.ShapeDtypeStruct((t, d), F32),
        scratch_shapes=[pltpu.VMEM((CONV_HALO + ts, wb), F32)] * 2,
        compiler_params=_params(2),
        name="conv_out",
    )(x2, oa, hc, hc, g, conv_w, conv_vecs, w_out)


def _sgu_out_kernel(x_ref, u_ref, vc_ref, od_ref, g_ref, lnv_ref, ws_ref, bs_ref, w_ref, o_ref,
                    oc_ref):
    ts = x_ref.shape[0]
    n_chunk = ts // SGU_CHUNK
    gw = vc_ref.shape[1] // C_GROUPS
    v = _layernorm(_gelu_tanh(vc_ref[...].astype(F32)), lnv_ref[0:1, :],
                   lnv_ref[1:2, :]).astype(BF16)
    row = lax.broadcasted_iota(jnp.int32, (SGU_CHUNK, SGU_CHUNK), 0)
    col = lax.broadcasted_iota(jnp.int32, (SGU_CHUNK, SGU_CHUNK), 1)
    for grp in range(C_GROUPS):
        w_s = jnp.where(col <= row, ws_ref[grp], 0.0).astype(BF16)
        lanes = slice(grp * gw, (grp + 1) * gw)
        vg = jnp.concatenate(
            [v[n * SGU_CHUNK:(n + 1) * SGU_CHUNK, lanes] for n in range(n_chunk)], axis=1)
        mixed = jnp.dot(w_s, vg, preferred_element_type=F32) + bs_ref[:, grp:grp + 1]
        for n in range(n_chunk):
            rows = slice(n * SGU_CHUNK, (n + 1) * SGU_CHUNK)
            oc_ref[rows, lanes] = (_gelu_tanh(u_ref[rows, lanes].astype(F32))
                                   * mixed[:, n * gw:(n + 1) * gw]).astype(BF16)
    wc = u_ref.shape[1]
    m = (jnp.dot(oc_ref[...], w_ref[0:wc, :], preferred_element_type=F32)
         + jnp.dot(od_ref[...], w_ref[wc:, :], preferred_element_type=F32))
    o_ref[...] = x_ref[...] + _rms(m, g_ref[3:4, :])


def _sgu_out(x2, u, vc, od, g, ln_vecs, w_s, b_s_t, w_out, layer, ts=1024):
    t, d = x2.shape
    wc = u.shape[1]
    par = (layer // 2,)
    tile = lambda width: pl.BlockSpec((ts, width), lambda i: (i, 0))
    return pl.pallas_call(
        _sgu_out_kernel,
        grid=(t // ts,),
        in_specs=[tile(d), tile(wc), tile(wc), tile(od.shape[1]),
                  _resident(g, (layer,)), _resident(ln_vecs, par), _resident(w_s, par),
                  _resident(b_s_t, par), _resident(w_out, par)],
        out_specs=tile(d),
        out_shape=jax.ShapeDtypeStruct((t, d), F32),
        scratch_shapes=[pltpu.VMEM((ts, wc), BF16)],
        compiler_params=_params(1),
        name="sgu_out",
    )(x2, u, vc, od, g, ln_vecs, w_s, b_s_t, w_out)


def _rope_tables(seq):
    half = HEAD_DIM // 2
    pos = jnp.arange(seq, dtype=F32)
    inv = ROPE_THETA ** (-jnp.arange(0, HEAD_DIM, 2, dtype=F32) / HEAD_DIM)
    ang = pos[:, None] * inv[None, :]
    cos, sin = jnp.cos(ang), jnp.sin(ang)
    reps = LANES // HEAD_DIM
    cos_t = jnp.tile(jnp.concatenate([cos, cos], axis=1), (1, reps))
    sin_t = jnp.tile(jnp.concatenate([-sin, sin], axis=1), (1, reps))
    return cos_t, sin_t


def kernel(x, norm_g, ffn_w_gate, ffn_w_up, ffn_w_down, ab_w_in, ab_w_out, conv_w, conv_b,
           conv_ln_g, conv_ln_b, cd_w_in, cd_w_out, sgu_ln_g, sgu_ln_b, sgu_w, sgu_b):
    bsz, seq, d = x.shape
    depth = norm_g.shape[0]
    rope_tabs = _rope_tables(seq)
    wg, wu, wd = ffn_w_gate, ffn_w_up, ffn_w_down
    ab_in, cd_in = ab_w_in, cd_w_in
    ab_out, cd_out = ab_w_out.astype(BF16), cd_w_out.astype(BF16)
    conv_vecs = jnp.stack([conv_b, conv_ln_g, conv_ln_b], axis=1)
    ln_vecs = jnp.stack([sgu_ln_g, sgu_ln_b], axis=1)
    sgu_b_t = jnp.swapaxes(sgu_b, 1, 2)
    x2 = x.reshape(bsz * seq, d)
    for layer in range(depth):
        x2 = _ffn(x2, norm_g, wg, wu, wd, layer, 0)
        if layer % 2 == 0:
            q, k, v, hc = _inproj(x2, norm_g, ab_in, layer, seq, rope_tabs,
                                  [F32, F32, BF16, BF16])
            oa = _attention("moba", q, k, v, bsz, seq)
            x2 = _conv_out(x2, oa, hc, norm_g, conv_w, conv_vecs, ab_out, layer, bsz, seq)
        else:
            u, vc, q, k, v = _inproj(x2, norm_g, cd_in, layer, seq, None, [BF16] * 5)
            od = _attention("stick", q, k, v, bsz, seq)
            x2 = _sgu_out(x2, u, vc, od, norm_g, ln_vecs, sgu_w, sgu_b_t, cd_out, layer)
        x2 = _ffn(x2, norm_g, wg, wu, wd, layer, 1)
    return x2.reshape(bsz, seq, d)
```

```python
import functools

import jax
import jax.numpy as jnp
import numpy as np
from jax import lax
from jax.experimental import pallas as pl
from jax.experimental.pallas import tpu as pltpu

HEAD_DIM = 64
MOBA_BLOCK = 256
MOBA_TOPK = 3
CONV_K = 31
SGU_CHUNK = 128
C_GROUPS = 4
ROPE_THETA = 10000.0
RMS_EPS = 1e-6
LN_EPS = 1e-5
LOG2E = 1.4426950408889634

LANES = 128
SUBLANES = 8
CONV_HALO = 32
MOBA_ROW_CHUNK = 32
STICK_TILE = 256
STICK_DEAD_BELOW = -110.0
VMEM_LIMIT = 60 * 1024 * 1024

F32 = jnp.float32
BF16 = jnp.bfloat16


def _rms(x, g):
    return x * lax.rsqrt(jnp.mean(x * x, axis=-1, keepdims=True) + RMS_EPS) * g


def _layernorm(x, g, b):
    mu = jnp.mean(x, axis=-1, keepdims=True)
    xc = x - mu
    var = jnp.mean(xc * xc, axis=-1, keepdims=True)
    return xc * lax.rsqrt(var + LN_EPS) * g + b


def _sigmoid(x):
    return 1.0 / (1.0 + jnp.exp(-x))


def _gelu_tanh(x):
    c = 0.7978845608028654
    return 0.5 * x * (1.0 + jnp.tanh(c * (x + 0.044715 * (x * x * x))))


def _resident(arr, lead):
    tail = arr.shape[len(lead):]
    index = tuple(lead) + (0,) * len(tail)
    return pl.BlockSpec((None,) * len(lead) + tail, lambda *_: index,
                        pipeline_mode=pl.Buffered(1))


def _params(n_grid):
    return pltpu.CompilerParams(
        dimension_semantics=("arbitrary",) * n_grid, vmem_limit_bytes=VMEM_LIMIT)


def _ffn_kernel(x_ref, g_ref, wg_ref, wu_ref, wd_ref, o_ref, act_ref, *, g_in, g_out, tf):
    n_sub, ts, d_ff = act_ref.shape

    def rows(s):
        return slice(s * ts, (s + 1) * ts)

    def normed_input(s):
        return _rms(x_ref[rows(s), :], g_ref[g_in:g_in + 1, :]).astype(BF16)

    def finish(s, f):
        o_ref[rows(s), :] = x_ref[rows(s), :] + 0.5 * _rms(f, g_ref[g_out:g_out + 1, :])

    h = normed_input(0)
    f_prev = None
    for s in range(n_sub):
        h_next = None
        for c in range(d_ff // tf):
            sl = slice(c * tf, (c + 1) * tf)
            gate = jnp.dot(h, wg_ref[:, sl].astype(BF16), preferred_element_type=F32)
            up = jnp.dot(h, wu_ref[:, sl].astype(BF16), preferred_element_type=F32)
            act_ref[s, :, sl] = (gate * _sigmoid(gate) * up).astype(BF16)
            if c == 0:
                if s + 1 < n_sub:
                    h_next = normed_input(s + 1)
                if f_prev is not None:
                    finish(s - 1, f_prev)
        f_prev = jnp.dot(act_ref[s], wd_ref[...].astype(BF16), preferred_element_type=F32)
        h = h_next
    finish(n_sub - 1, f_prev)


def _ffn(x2, g, wg, wu, wd, layer, half, tm=1024, n_sub=2, tf=256):
    t, d = x2.shape
    d_ff = wg.shape[-1]
    g_in, g_out = (0, 1) if half == 0 else (4, 5)
    return pl.pallas_call(
        functools.partial(_ffn_kernel, g_in=g_in, g_out=g_out, tf=tf),
        grid=(t // tm,),
        in_specs=[
            pl.BlockSpec((tm, d), lambda i: (i, 0)),
            _resident(g, (layer,)),
            _resident(wg, (layer, half)),
            _resident(wu, (layer, half)),
            _resident(wd, (layer, half)),
        ],
        out_specs=pl.BlockSpec((tm, d), lambda i: (i, 0)),
        out_shape=jax.ShapeDtypeStruct((t, d), F32),
        scratch_shapes=[pltpu.VMEM((n_sub, tm // n_sub, d_ff), BF16)],
        compiler_params=_params(1),
        name="ffn",
    )(x2, g, wg, wu, wd)


def _rope(x, cos, sin_signed):
    half = HEAD_DIM // 2
    lane = lax.broadcasted_iota(jnp.int32, x.shape, 1)
    first_half = (lane % HEAD_DIM) < half
    partner = jnp.where(first_half, pltpu.roll(x, LANES - half, 1), pltpu.roll(x, half, 1))
    return x * cos + partner * sin_signed


def _inproj_kernel(x_ref, g_ref, w_ref, *rest, rope, width):
    if rope:
        cos_ref, sin_ref = rest[:2]
        outs = rest[2:]
    else:
        lnv_ref, outs = rest[0], rest[1:]
    h = _rms(x_ref[...], g_ref[2:3, :]).astype(BF16)
    proj = lambda n: jnp.dot(h, w_ref[:, n * width:(n + 1) * width].astype(BF16),
                             preferred_element_type=F32)
    for n, o_ref in enumerate(outs):
        if rope and n == 3:
            o_ref[...] = (proj(3) * _sigmoid(proj(4))).astype(o_ref.dtype)
            break
        y = proj(n)
        if not rope and n < 2:
            y = _gelu_tanh(y)
            if n == 1:
                y = _layernorm(y, lnv_ref[0:1, :], lnv_ref[1:2, :])
        if rope and n < 2:
            for t in range(width // LANES):
                sl = slice(t * LANES, (t + 1) * LANES)
                o_ref[:, sl] = _rope(y[:, sl], cos_ref[...], sin_ref[...]).astype(o_ref.dtype)
        else:
            o_ref[...] = y.astype(o_ref.dtype)


def _inproj(x2, g, w, layer, seq, rope_tabs, out_dtypes, ln_vecs=None, tm=1024):
    t, d = x2.shape
    width = w.shape[-1] // 5
    n_seq = seq // tm
    in_specs = [
        pl.BlockSpec((tm, d), lambda i: (i, 0)),
        _resident(g, (layer,)),
        _resident(w, (layer // 2,)),
    ]
    args = [x2, g, w]
    if rope_tabs is not None:
        in_specs += [pl.BlockSpec((tm, LANES), lambda i: (i % n_seq, 0))] * 2
        args += list(rope_tabs)
    else:
        in_specs.append(_resident(ln_vecs, (layer // 2,)))
        args.append(ln_vecs)
    return pl.pallas_call(
        functools.partial(_inproj_kernel, rope=rope_tabs is not None, width=width),
        grid=(t // tm,),
        in_specs=in_specs,
        out_specs=[pl.BlockSpec((tm, width), lambda i: (i, 0))] * len(out_dtypes),
        out_shape=[jax.ShapeDtypeStruct((t, width), dt) for dt in out_dtypes],
        compiler_params=_params(1),
        name="inproj_rope" if rope_tabs is not None else "inproj",
    )(*args)


def _moba_select_bias(gate_t, own):
    sub = lax.broadcasted_iota(jnp.int32, gate_t.shape, 0)
    past = sub < own
    gm = jnp.where(past, gate_t, -jnp.inf)
    rank = jnp.zeros(gate_t.shape, jnp.int32)
    for i in range(own):
        gi = gm[i:i + 1, :]
        beats = (gi > gm) | ((gi == gm) & (sub > i))
        rank = rank + jnp.where(beats, 1, 0)
    return jnp.where(past & (rank < MOBA_TOPK), 0.0, -jnp.inf)


def _moba_kernel(q_ref, k_ref, v_ref, o_ref, qm_ref, qs_ref, kb_ref, vt_ref, km_ref, s_ref,
                 p_ref):
    seq = q_ref.shape[1]
    blk = MOBA_BLOCK
    n_blk = seq // blk
    n_sub = 8 * pl.cdiv(n_blk, 8)
    scale = HEAD_DIM ** -0.5
    nt = (((1,), (1,)), ((), ()))
    kb_ref[...] = k_ref[0].astype(BF16)
    vt_ref[0:LANES, :] = v_ref[0].astype(F32).T.astype(BF16)
    vt_ref[LANES:, :] = jnp.ones((2 * SUBLANES, seq), BF16)
    km_ref[...] = jnp.zeros(km_ref.shape, F32)
    for j in range(n_blk):
        km_ref[j:j + 1, :] = jnp.mean(k_ref[0, j * blk:(j + 1) * blk, :], axis=0, keepdims=True)
    lane = lax.broadcasted_iota(jnp.int32, (1, LANES), 1)
    for hh in range(2):
        qm = jnp.where(lane // HEAD_DIM == hh, q_ref[0], 0.0)
        qm_ref[hh] = qm
        qs_ref[hh] = (qm * (scale * LOG2E)).astype(BF16)
    key = lax.broadcasted_iota(jnp.int32, (blk, blk), 0)
    qry = lax.broadcasted_iota(jnp.int32, (blk, blk), 1)
    causal_bias = jnp.where(key <= qry, 0.0, -jnp.inf)
    head_rows = lax.broadcasted_iota(jnp.int32, (LANES, 1), 0) < HEAD_DIM

    def scores(i, hh):
        rows = slice(i * blk, (i + 1) * blk)
        n_keys = (i + 1) * blk
        s_ref[2 * (i % 2) + hh, 0:n_keys, :] = lax.dot_general(
            kb_ref[0:n_keys, :], qs_ref[hh, rows, :], nt, preferred_element_type=F32)
        if i <= MOBA_TOPK:
            return None
        return lax.dot_general(km_ref[...], qm_ref[hh, rows, :], nt,
                               precision=lax.Precision.HIGHEST,
                               preferred_element_type=F32)[0:n_sub]

    def softmax(i, hh, gate_t):
        buf = 2 * (i % 2) + hh
        biases = [None] * i + [causal_bias]
        if gate_t is not None:
            bias = _moba_select_bias(gate_t, i)
            biases = [bias[j:j + 1, :] for j in range(i)] + [causal_bias]

        def chunks(j):
            for r in range(0, blk, MOBA_ROW_CHUNK):
                yield r, slice(j * blk + r, j * blk + r + MOBA_ROW_CHUNK)

        m = None
        for j, b in enumerate(biases):
            m_j = None
            for r, rows in chunks(j):
                x = s_ref[buf, rows, :]
                if j == i:
                    x = x + causal_bias[r:r + MOBA_ROW_CHUNK]
                m_j = x if m_j is None else jnp.maximum(m_j, x)
            m_j = jnp.max(m_j, axis=0, keepdims=True)
            if j < i and b is not None:
                m_j = m_j + b
            m = m_j if m is None else jnp.maximum(m, m_j)
        for j, b in enumerate(biases):
            for r, rows in chunks(j):
                if j == i:
                    x = s_ref[buf, rows, :] + (causal_bias[r:r + MOBA_ROW_CHUNK] - m)
                else:
                    x = s_ref[buf, rows, :] - (m if b is None else m - b)
                p_ref[hh, rows, :] = jnp.exp2(x).astype(BF16)

    def weighted_values(i, hh):
        n_keys = (i + 1) * blk
        o_t = jnp.dot(vt_ref[:, 0:n_keys], p_ref[hh, 0:n_keys, :], preferred_element_type=F32)
        return o_t[0:LANES] * (1.0 / o_t[LANES:LANES + 1])

    pending = [scores(0, hh) for hh in range(2)]
    for i in range(n_blk):
        gates = pending
        if i + 1 < n_blk:
            pending = [scores(i + 1, hh) for hh in range(2)]
        for hh in range(2):
            softmax(i, hh, gates[hh])
        outs = [weighted_values(i, hh) for hh in range(2)]
        o_t = jnp.where(head_rows, outs[0], outs[1])
        o_ref[0, i * blk:(i + 1) * blk, :] = o_t.T.astype(o_ref.dtype)


def _log_one_minus_beta(zn):
    e = jnp.exp2(jnp.abs(zn) * (-LOG2E))
    return jnp.minimum(zn, 0.0) - jnp.log(1.0 + e)


def _suffix_sums(x, tri2):
    hi = x.astype(BF16)
    lo = (x - hi.astype(F32)).astype(BF16)
    return jnp.dot(jnp.concatenate([hi, lo], axis=1), tri2, preferred_element_type=F32)


def _stick_tiles(n_q):
    entries, first = [], [0]
    for d in range(n_q):
        n_par = n_q if d == 0 else n_q - 1
        group = [(i, i - d) for i in range(d, n_q)]
        entries += group + [(n_q, 0)] * (-len(group) % n_par)
        first.append(len(entries))
    return np.array(entries, np.int32).T, np.array(first, np.int32)


def _stick_kernel(tab_ref, first_ref, q_ref, k_ref, v_ref, o_ref, qn_ref, acc_ref, carry_ref,
                  live_ref, *, tq):
    seq = q_ref.shape[1]
    n_q = seq // tq
    scale = HEAD_DIM ** -0.5
    nt = (((1,), (1,)), ((), ()))
    lane = lax.broadcasted_iota(jnp.int32, (1, LANES), 1)
    q_neg = q_ref[0].astype(F32) * (-scale)
    spare = pl.ds(seq, tq)
    for hh in range(2):
        qn_ref[hh, 0:seq, :] = jnp.where(lane // HEAD_DIM == hh, q_neg, 0.0).astype(BF16)
        qn_ref[hh, spare, :] = jnp.zeros((tq, LANES), BF16)
        acc_ref[hh, spare, :] = jnp.zeros((tq, LANES), F32)
        carry_ref[hh, spare, :] = jnp.zeros((tq, LANES), F32)
    row = lax.broadcasted_iota(jnp.int32, (tq, tq), 0)
    col = lax.broadcasted_iota(jnp.int32, (tq, tq), 1)
    strict = col < row
    tri = (row > col).astype(BF16)
    tri2 = jnp.concatenate([tri, tri], axis=0)

    def tiles_per_step(diag):
        return n_q if diag else n_q - 1

    def process(first, diag):
        chains = []
        for t in range(tiles_per_step(diag)):
            q_start = pl.multiple_of(tab_ref[0, first + t] * tq, tq)
            k_start = pl.multiple_of(tab_ref[1, first + t] * tq, tq)
            kc = k_ref[0, pl.ds(k_start, tq), :]
            vc = v_ref[0, pl.ds(k_start, tq), :]
            for hh in range(2):
                rows = (hh, pl.ds(q_start, tq), slice(None))
                zn = lax.dot_general(qn_ref[rows], kc, nt, preferred_element_type=F32)
                chains.append(dict(rows=rows, vc=vc, zn=zn, more_left=k_start > 0))
        for ch in chains:
            log_1m = _log_one_minus_beta(ch["zn"])
            ch["log_1m"] = jnp.where(strict, log_1m, 0.0) if diag else log_1m
        for ch in chains:
            total = jnp.broadcast_to(jnp.sum(ch["log_1m"], axis=1, keepdims=True), (tq, LANES))
            if diag:
                carry_ref[ch["rows"]] = total
            else:
                ch["carry"] = carry_ref[ch["rows"]]
                new_carry = ch["carry"] + total
                carry_ref[ch["rows"]] = new_carry
                top = jnp.max(new_carry, axis=0, keepdims=True)
                live_ref[...] = jnp.maximum(live_ref[...],
                                            jnp.where(ch["more_left"], top, -jnp.inf))
        for ch in chains:
            ch["after"] = _suffix_sums(ch["log_1m"], tri2)
        for ch in chains:
            log_a = (ch["log_1m"] - ch["zn"]) + ch["after"]
            if not diag:
                log_a = log_a + jnp.concatenate([ch["carry"]] * (tq // LANES), axis=1)
            att = jnp.exp2(log_a * LOG2E)
            if diag:
                att = jnp.where(strict, att, 0.0)
            ch["att"] = att.astype(BF16)
        for ch in chains:
            out = jnp.dot(ch["att"], ch["vc"], preferred_element_type=F32)
            if diag:
                acc_ref[ch["rows"]] = out
            else:
                acc_ref[ch["rows"]] += out

    def steps(d, diag):
        first = first_ref[d]
        n_par = tiles_per_step(diag)

        def step(it, _):
            process(first + it * n_par, diag)
            return 0

        lax.fori_loop(0, (first_ref[d + 1] - first) // n_par, step, 0)

    def more_to_do(state):
        d, live = state
        return (d < n_q) & (live > STICK_DEAD_BELOW)

    def distance(state):
        d, _ = state
        live_ref[...] = jnp.full(live_ref.shape, -jnp.inf, F32)
        steps(d, False)
        return d + 1, jnp.max(live_ref[...])

    steps(0, True)
    lax.while_loop(more_to_do, distance, (jnp.int32(1), jnp.float32(0.0)))
    o_ref[0] = jnp.where(lane < HEAD_DIM, acc_ref[0, 0:seq, :],
                         acc_ref[1, 0:seq, :]).astype(o_ref.dtype)


def _attention(kind, q, k, v, bsz, seq):
    width = q.shape[1]
    q3, k3, v3 = (a.reshape(bsz, seq, width) for a in (q, k, v))
    spec = pl.BlockSpec((1, seq, LANES), lambda b, p: (b, 0, p))
    if kind == "moba":
        body = _moba_kernel
        scratch = [pltpu.VMEM((2, seq, LANES), F32),
                   pltpu.VMEM((2, seq, LANES), BF16),
                   pltpu.VMEM((seq, LANES), BF16),
                   pltpu.VMEM((LANES + 2 * SUBLANES, seq), BF16),
                   pltpu.VMEM((LANES, LANES), F32),
                   pltpu.VMEM((4, seq, MOBA_BLOCK), F32),
                   pltpu.VMEM((2, seq, MOBA_BLOCK), BF16)]
        in_specs, args = [spec, spec, spec], (q3, k3, v3)
    else:
        tq = STICK_TILE
        tiles, first = _stick_tiles(seq // tq)
        body = functools.partial(_stick_kernel, tq=tq)
        assert q.dtype == k.dtype == v.dtype == BF16
        scratch = [pltpu.VMEM((2, seq + tq, LANES), BF16),
                   pltpu.VMEM((2, seq + tq, LANES), F32),
                   pltpu.VMEM((2, seq + tq, LANES), F32),
                   pltpu.VMEM((1, LANES), F32)]
        smem = pl.BlockSpec(memory_space=pltpu.SMEM)
        in_specs = [smem, smem, spec, spec, spec]
        args = (jnp.asarray(tiles), jnp.asarray(first), q3, k3, v3)
    out = pl.pallas_call(
        body,
        grid=(bsz, width // LANES),
        in_specs=in_specs,
        out_specs=spec,
        out_shape=jax.ShapeDtypeStruct((bsz, seq, width), BF16),
        scratch_shapes=scratch,
        compiler_params=_params(2),
        name=kind,
    )(*args)
    return out.reshape(bsz * seq, width)


def _conv_out_kernel(x_ref, oa_ref, hc_ref, halo_ref, g_ref, cw_ref, cvec_ref, w_ref, o_ref,
                     ext_ref, win_ref):
    ts = x_ref.shape[0]
    first = pl.program_id(1) == 0
    ext_ref[0:CONV_HALO, :] = jnp.where(first, 0.0, halo_ref[...].astype(F32))
    ext_ref[CONV_HALO:CONV_HALO + ts, :] = hc_ref[...].astype(F32)
    off = CONV_HALO - (CONV_K - 1)
    acc = jnp.zeros((ts, hc_ref.shape[1]), F32) + cvec_ref[0:1, :]
    for phase in range(SUBLANES):
        taps = [tap for tap in range(CONV_K) if (off + tap) % SUBLANES == phase]
        if not taps:
            continue
        span = (off + taps[-1]) // SUBLANES * SUBLANES + ts
        win_ref[0:span, :] = ext_ref[phase:phase + span, :]
        for tap in taps:
            start = (off + tap) // SUBLANES * SUBLANES
            acc = acc + win_ref[start:start + ts, :] * cw_ref[tap:tap + 1, :]
    y = _layernorm(acc, cvec_ref[1:2, :], cvec_ref[2:3, :])
    ob = y * _sigmoid(y)
    wa = oa_ref.shape[1]
    m = (jnp.dot(oa_ref[...], w_ref[0:wa, :], preferred_element_type=F32)
         + jnp.dot(ob.astype(BF16), w_ref[wa:, :], preferred_element_type=F32))
    o_ref[...] = x_ref[...] + _rms(m, g_ref[3:4, :])


def _conv_out(x2, oa, hc, g, conv_w, conv_vecs, w_out, layer, bsz, seq, ts=1024):
    t, d = x2.shape
    par = (layer // 2,)
    wb = hc.shape[1]
    n_seq = seq // ts
    halo_per_tile = ts // CONV_HALO
    tile = lambda width: pl.BlockSpec((ts, width), lambda b, i: (b * n_seq + i, 0))
    halo = pl.BlockSpec(
        (CONV_HALO, wb),
        lambda b, i: (jnp.maximum((b * n_seq + i) * halo_per_tile - 1, 0), 0))
    return pl.pallas_call(
        _conv_out_kernel,
        grid=(bsz, n_seq),
        in_specs=[tile(d), tile(oa.shape[1]), tile(wb), halo,
                  _resident(g, (layer,)), _resident(conv_w, par), _resident(conv_vecs, par),
                  _resident(w_out, par)],
        out_specs=tile(d),
        out_shape=jax.ShapeDtypeStruct((t, d), F32),
        scratch_shapes=[pltpu.VMEM((CONV_HALO + ts, wb), F32)] * 2,
        compiler_params=_params(2),
        name="conv_out",
    )(x2, oa, hc, hc, g, conv_w, conv_vecs, w_out)


def _sgu_out_kernel(x_ref, u_ref, vc_ref, od_ref, g_ref, ws_ref, bs_ref, w_ref, o_ref, oc_ref):
    ts = x_ref.shape[0]
    n_chunk = ts // SGU_CHUNK
    gw = vc_ref.shape[1] // C_GROUPS
    v = vc_ref[...]
    row = lax.broadcasted_iota(jnp.int32, (SGU_CHUNK, SGU_CHUNK), 0)
    col = lax.broadcasted_iota(jnp.int32, (SGU_CHUNK, SGU_CHUNK), 1)
    for grp in range(C_GROUPS):
        w_s = jnp.where(col <= row, ws_ref[grp], 0.0).astype(BF16)
        lanes = slice(grp * gw, (grp + 1) * gw)
        vg = jnp.concatenate(
            [v[n * SGU_CHUNK:(n + 1) * SGU_CHUNK, lanes] for n in range(n_chunk)], axis=1)
        mixed = jnp.dot(w_s, vg, preferred_element_type=F32) + bs_ref[:, grp:grp + 1]
        for n in range(n_chunk):
            rows = slice(n * SGU_CHUNK, (n + 1) * SGU_CHUNK)
            oc_ref[rows, lanes] = (u_ref[rows, lanes].astype(F32)
                                   * mixed[:, n * gw:(n + 1) * gw]).astype(BF16)
    wc = u_ref.shape[1]
    m = (jnp.dot(oc_ref[...], w_ref[0:wc, :], preferred_element_type=F32)
         + jnp.dot(od_ref[...], w_ref[wc:, :], preferred_element_type=F32))
    o_ref[...] = x_ref[...] + _rms(m, g_ref[3:4, :])


def _sgu_out(x2, u, vc, od, g, w_s, b_s_t, w_out, layer, ts=1024):
    t, d = x2.shape
    wc = u.shape[1]
    par = (layer // 2,)
    tile = lambda width: pl.BlockSpec((ts, width), lambda i: (i, 0))
    return pl.pallas_call(
        _sgu_out_kernel,
        grid=(t // ts,),
        in_specs=[tile(d), tile(wc), tile(wc), tile(od.shape[1]),
                  _resident(g, (layer,)), _resident(w_s, par),
                  _resident(b_s_t, par), _resident(w_out, par)],
        out_specs=tile(d),
        out_shape=jax.ShapeDtypeStruct((t, d), F32),
        scratch_shapes=[pltpu.VMEM((ts, wc), BF16)],
        compiler_params=_params(1),
        name="sgu_out",
    )(x2, u, vc, od, g, w_s, b_s_t, w_out)


def _rope_tables(seq):
    half = HEAD_DIM // 2
    pos = jnp.arange(seq, dtype=F32)
    inv = ROPE_THETA ** (-jnp.arange(0, HEAD_DIM, 2, dtype=F32) / HEAD_DIM)
    ang = pos[:, None] * inv[None, :]
    cos, sin = jnp.cos(ang), jnp.sin(ang)
    reps = LANES // HEAD_DIM
    cos_t = jnp.tile(jnp.concatenate([cos, cos], axis=1), (1, reps))
    sin_t = jnp.tile(jnp.concatenate([-sin, sin], axis=1), (1, reps))
    return cos_t, sin_t


def kernel(x, norm_g, ffn_w_gate, ffn_w_up, ffn_w_down, ab_w_in, ab_w_out, conv_w, conv_b,
           conv_ln_g, conv_ln_b, cd_w_in, cd_w_out, sgu_ln_g, sgu_ln_b, sgu_w, sgu_b):
    bsz, seq, d = x.shape
    depth = norm_g.shape[0]
    rope_tabs = _rope_tables(seq)
    wg, wu, wd = ffn_w_gate, ffn_w_up, ffn_w_down
    ab_in, cd_in = ab_w_in, cd_w_in
    ab_out, cd_out = ab_w_out.astype(BF16), cd_w_out.astype(BF16)
    conv_vecs = jnp.stack([conv_b, conv_ln_g, conv_ln_b], axis=1)
    ln_vecs = jnp.stack([sgu_ln_g, sgu_ln_b], axis=1)
    sgu_b_t = jnp.swapaxes(sgu_b, 1, 2)
    x2 = x.reshape(bsz * seq, d)
    for layer in range(depth):
        x2 = _ffn(x2, norm_g, wg, wu, wd, layer, 0)
        if layer % 2 == 0:
            q, k, v, hc = _inproj(x2, norm_g, ab_in, layer, seq, rope_tabs,
                                  [F32, F32, BF16, BF16])
            oa = _attention("moba", q, k, v, bsz, seq)
            x2 = _conv_out(x2, oa, hc, norm_g, conv_w, conv_vecs, ab_out, layer, bsz, seq)
        else:
            u, vc, q, k, v = _inproj(x2, norm_g, cd_in, layer, seq, None, [BF16] * 5, ln_vecs)
            od = _attention("stick", q, k, v, bsz, seq)
            x2 = _sgu_out(x2, u, vc, od, norm_g, sgu_w, sgu_b_t, cd_out, layer)
        x2 = _ffn(x2, norm_g, wg, wu, wd, layer, 1)
    return x2.reshape(bsz, seq, d)
```
